```python
import math
import jax, jax.numpy as jnp
from jax import lax
import numpy as np

D_MODEL = 1024
BATCH = 16
SEQ = 2048
DEPTH = 4

CHUNK = 64
N_EVEN = (DEPTH + 1) // 2
N_ODD = DEPTH // 2
DEEPNORM_ALPHA = (2.0 * DEPTH) ** 0.25
DEEPNORM_BETA = (8.0 * DEPTH) ** -0.25
D_FF = 2816
N_SUB = 3
LN_EPS = 1e-5
RMS_EPS = 1e-6
NEG_INF = -1e30

A_HEADS = 8
A_HEAD_DIM = 64
A_WIDTH = A_HEADS * A_HEAD_DIM
IDX_HEADS = 8
IDX_DIM = 64
TOPK_MAX = 256
SPARSE_Q_BLOCK = 32

B_HEADS = 4
B_HEAD_DIM = 128
B_WIDTH = B_HEADS * B_HEAD_DIM
CONV_K = 4

EVEN_SIZES = (A_WIDTH, A_WIDTH, A_WIDTH, IDX_HEADS * IDX_DIM, IDX_DIM, IDX_HEADS,
              3 * B_WIDTH, B_HEADS, B_HEADS, B_WIDTH)
EVEN_IN = sum(EVEN_SIZES)
EVEN_SPLITS = tuple(int(s) for s in np.cumsum(EVEN_SIZES)[:-1])
EVEN_MIX = A_WIDTH + B_WIDTH

C_HEADS = 16
C_Q_RANK = 384
C_KV_RANK = 256
C_NOPE = 64
C_ROPE = 32
C_V = 64
C_IN = C_Q_RANK + C_KV_RANK + C_ROPE
C_MIX = C_HEADS * C_V
ROPE_THETA = 10000.0
DENSE_Q_BLOCK = 128

kernel_name = "hybrid_dsa_gdn_mla_deepnorm_adaln"


def layer_norm(x, g, b):
    xf = x.astype(jnp.float32)
    mu = jnp.mean(xf, -1, keepdims=True)
    var = jnp.mean(jnp.square(xf - mu), -1, keepdims=True)
    return ((xf - mu) * lax.rsqrt(var + LN_EPS)).astype(x.dtype) * g + b


def rms_norm(x, g):
    xf = x.astype(jnp.float32)
    return (xf * lax.rsqrt(jnp.mean(xf * xf, -1, keepdims=True) + RMS_EPS)).astype(x.dtype) * g


def l2_normalize(x):
    return x * lax.rsqrt(jnp.sum(x * x, -1, keepdims=True) + RMS_EPS)


def swiglu(u, w_gate, w_up, w_down):
    return (jax.nn.silu(u @ w_gate) * (u @ w_up)) @ w_down


def alibi_slopes(n):
    return jnp.asarray(2.0 ** (-8.0 * np.arange(1, n + 1) / n), dtype=jnp.float32)


def rope(x, pos):
    half = x.shape[-1] // 2
    inv_freq = ROPE_THETA ** (-jnp.arange(half, dtype=jnp.float32) / half)
    ang = pos.astype(jnp.float32)[:, :, None] * inv_freq
    cos, sin = jnp.cos(ang)[:, :, None, :], jnp.sin(ang)[:, :, None, :]
    xf = x.astype(jnp.float32)
    x1, x2 = xf[..., :half], xf[..., half:]
    return jnp.concatenate([x1 * cos - x2 * sin, x1 * sin + x2 * cos], -1).astype(x.dtype)


def to_query_blocks(t, size):
    return t.reshape(t.shape[0], t.shape[1] // size, size, *t.shape[2:]).swapaxes(0, 1)


def from_query_blocks(t):
    t = t.swapaxes(0, 1)
    return t.reshape(t.shape[0], t.shape[1] * t.shape[2], -1)


def dsa_attention(q, k, v, iq, ik, iw, pos):
    top_k = min(TOPK_MAX, pos.shape[1] // 4)
    key_chunk = pos // CHUNK
    slopes = alibi_slopes(A_HEADS)
    ik32 = ik.astype(jnp.float32)
    iw32 = iw.astype(jnp.float32) * IDX_HEADS ** -0.5
    gather = jax.vmap(lambda table, idx: table[idx])

    def block(args):
        qb, iqb, iwb, pb = args
        qchunk = pb // CHUNK
        dots = jnp.einsum('bqhd,bsd->bqhs', iqb.astype(jnp.float32), ik32) * IDX_DIM ** -0.5
        score = jnp.einsum('bqh,bqhs->bqs', iwb, jax.nn.relu(dots))
        visible = key_chunk[:, None, :] <= qchunk[:, :, None]
        score = jnp.where(visible, score, NEG_INF)
        _, sel = lax.top_k(score, top_k)
        k_sel = gather(k, sel)
        v_sel = gather(v, sel)
        pos_sel = gather(pos, sel)
        valid = (pos_sel // CHUNK) <= qchunk[:, :, None]
        dist = jnp.abs(pb[:, :, None] - pos_sel).astype(jnp.float32)
        logits = jnp.einsum('bqhd,bqkhd->bhqk', qb, k_sel,
                            preferred_element_type=jnp.float32) * A_HEAD_DIM ** -0.5
        logits = logits - slopes[None, :, None, None] * dist[:, None]
        p = jax.nn.softmax(jnp.where(valid[:, None], logits, NEG_INF), axis=-1)
        return jnp.einsum('bhqk,bqkhd->bqhd', p.astype(v.dtype), v_sel)

    out = lax.map(block, (to_query_blocks(q, SPARSE_Q_BLOCK), to_query_blocks(iq, SPARSE_Q_BLOCK),
                          to_query_blocks(iw32, SPARSE_Q_BLOCK), to_query_blocks(pos, SPARSE_Q_BLOCK)))
    return from_query_blocks(out)


def causal_depthwise_conv(x, w):
    return lax.conv_general_dilated(x, w[:, None, :].astype(x.dtype), window_strides=(1,),
                                    padding=[(CONV_K - 1, 0)],
                                    dimension_numbers=('NWC', 'WIO', 'NWC'),
                                    feature_group_count=x.shape[-1])


def gated_delta_rule(q, k, v, g, beta):
    bsz, seq, nh, dk = q.shape
    n = seq // CHUNK

    def to_chunks(t):
        t = t.reshape(bsz, n, CHUNK, nh, *t.shape[3:])
        return jnp.moveaxis(t, (1, 3), (0, 2))

    qc = to_chunks(q * dk ** -0.5)
    kc, vc, bc = to_chunks(k), to_chunks(v), to_chunks(beta)
    gc = jnp.cumsum(to_chunks(g), axis=-1)
    incl = jnp.tril(jnp.ones((CHUNK, CHUNK), dtype=bool))
    strict = jnp.tril(jnp.ones((CHUNK, CHUNK), dtype=bool), -1)
    decay = jnp.exp(jnp.where(incl, gc[..., :, None] - gc[..., None, :], -jnp.inf))
    kb = kc * bc[..., None]
    vb = vc * bc[..., None]
    m = jnp.where(strict, jnp.einsum('nbhid,nbhjd->nbhij', kb, kc) * decay, 0.0)
    eye = jnp.eye(CHUNK, dtype=jnp.float32)
    t_inv = lax.linalg.triangular_solve(eye + m, jnp.broadcast_to(eye, m.shape), left_side=True,
                                        lower=True, unit_diagonal=True)
    u = t_inv @ vb
    w = t_inv @ (kb * jnp.exp(gc)[..., None])
    attn_intra = jnp.where(incl, jnp.einsum('nbhid,nbhjd->nbhij', qc, kc) * decay, 0.0)

    def step(state, xs):
        q_i, k_i, u_i, w_i, g_i, a_i = xs
        v_new = u_i - w_i @ state
        o = (q_i * jnp.exp(g_i)[..., None]) @ state + a_i @ v_new
        g_last = g_i[..., -1]
        k_dec = k_i * jnp.exp(g_last[..., None] - g_i)[..., None]
        state = state * jnp.exp(g_last)[..., None, None] + jnp.einsum('bhcd,bhce->bhde', k_dec, v_new)
        return state, o

    s0 = jnp.zeros((bsz, nh, dk, v.shape[-1]), jnp.float32)
    _, o = lax.scan(step, s0, (qc, kc, u, w, gc, attn_intra))
    return jnp.moveaxis(o, (0, 2), (1, 3)).reshape(bsz, seq, nh, v.shape[-1])


def gdn_mixer(qkv, a, b, z, conv_w, a_log, dt_bias, norm_g):
    bsz, seq, _ = qkv.shape
    qkv = jax.nn.silu(causal_depthwise_conv(qkv, conv_w))
    q, k, v = jnp.split(qkv.astype(jnp.float32), 3, axis=-1)
    q = l2_normalize(q.reshape(bsz, seq, B_HEADS, B_HEAD_DIM))
    k = l2_normalize(k.reshape(bsz, seq, B_HEADS, B_HEAD_DIM))
    v = v.reshape(bsz, seq, B_HEADS, B_HEAD_DIM)
    g = -jnp.exp(a_log.astype(jnp.float32)) * jax.nn.softplus(a.astype(jnp.float32) + dt_bias.astype(jnp.float32))
    beta = jax.nn.sigmoid(b.astype(jnp.float32))
    o = gated_delta_rule(q, k, v, g, beta).astype(z.dtype)
    o = rms_norm(o, norm_g) * jax.nn.silu(z.reshape(bsz, seq, B_HEADS, B_HEAD_DIM))
    return o.reshape(bsz, seq, B_WIDTH)


def hybrid_mixer(u, pos, w_in, w_out, conv_w, a_log, dt_bias, norm_g):
    bsz, seq, _ = u.shape
    a_q, a_k, a_v, i_q, i_k, i_w, b_qkv, b_a, b_b, b_z = jnp.split(u @ w_in, EVEN_SPLITS, axis=-1)
    heads = lambda t, nh: t.reshape(bsz, seq, nh, -1)
    out_a = dsa_attention(heads(a_q, A_HEADS), heads(a_k, A_HEADS), heads(a_v, A_HEADS),
                          heads(i_q, IDX_HEADS), i_k, i_w, pos)
    out_b = gdn_mixer(b_qkv, b_a, b_b, b_z, conv_w, a_log, dt_bias, norm_g)
    return jnp.concatenate([out_a.astype(u.dtype), out_b], axis=-1) @ w_out


def chunk_causal_attention(q, k, v, pos, scale):
    key_chunk = pos // CHUNK

    def block(args):
        qb, pb = args
        logits = jnp.einsum('bqhd,bshd->bhqs', qb, k, preferred_element_type=jnp.float32) * scale
        visible = key_chunk[:, None, None, :] <= (pb // CHUNK)[:, None, :, None]
        p = jax.nn.softmax(jnp.where(visible, logits, NEG_INF), axis=-1)
        return jnp.einsum('bhqs,bshd->bqhd', p.astype(v.dtype), v)

    out = lax.map(block, (to_query_blocks(q, DENSE_Q_BLOCK), to_query_blocks(pos, DENSE_Q_BLOCK)))
    return from_query_blocks(out)


def mla_mixer(u, pos, w_in, q_norm_g, w_q_up, kv_norm_g, w_kv_up, w_out):
    bsz, seq, _ = u.shape
    c_q, c_kv, k_rope = jnp.split(u @ w_in, [C_Q_RANK, C_Q_RANK + C_KV_RANK], axis=-1)
    q = (rms_norm(c_q, q_norm_g) @ w_q_up).reshape(bsz, seq, C_HEADS, C_NOPE + C_ROPE)
    kv = (rms_norm(c_kv, kv_norm_g) @ w_kv_up).reshape(bsz, seq, C_HEADS, C_NOPE + C_V)
    q = jnp.concatenate([q[..., :C_NOPE], rope(q[..., C_NOPE:], pos)], axis=-1)
    k_rope = jnp.broadcast_to(rope(k_rope[:, :, None, :], pos), (bsz, seq, C_HEADS, C_ROPE))
    k = jnp.concatenate([kv[..., :C_NOPE], k_rope], axis=-1)
    v = kv[..., C_NOPE:]
    out = chunk_causal_attention(q, k, v, pos, (C_NOPE + C_ROPE) ** -0.5)
    return out @ w_out


def adaln_sublayer(x, m, f, ln_g, ln_b):
    shift, scale, gate = m[:, None, 0], m[:, None, 1], m[:, None, 2]
    y = f(x * (1 + scale) + shift)
    return layer_norm(DEEPNORM_ALPHA * x + (1 + gate) * y, ln_g, ln_b)


def setup_inputs(seed: int = 0) -> dict:
    key = jax.random.key(seed)
    ks = iter(jax.random.split(key, 32))

    def nrm(shape, fan_in, s=1.0):
        return jax.random.normal(next(ks), shape, jnp.float32) * (s * fan_in ** -0.5)

    def gain(shape):
        return 1.0 + 0.02 * jax.random.normal(next(ks), shape, jnp.float32)

    def small(shape, s=0.02):
        return s * jax.random.normal(next(ks), shape, jnp.float32)

    x = jax.random.normal(next(ks), (BATCH, SEQ, D_MODEL), jnp.float32)
    c = jax.random.normal(next(ks), (BATCH, D_MODEL), jnp.float32)
    offsets = jax.random.randint(next(ks), (BATCH, 1), 0, 1024) * CHUNK
    positions = (offsets + jnp.arange(SEQ)[None, :]).astype(jnp.int32)
    dt = jnp.exp(jax.random.uniform(next(ks), (N_EVEN, B_HEADS), jnp.float32,
                                    math.log(1e-3), math.log(1e-1)))
    return {
        "x": x,
        "c": c,
        "positions": positions,
        "mod_w": nrm((DEPTH, D_MODEL, N_SUB * 3 * D_MODEL), D_MODEL, 0.1),
        "mod_b": small((DEPTH, N_SUB * 3 * D_MODEL), 0.01),
        "ln_g": gain((DEPTH, N_SUB, D_MODEL)),
        "ln_b": small((DEPTH, N_SUB, D_MODEL)),
        "ffn_w_gate": nrm((DEPTH, 2, D_MODEL, D_FF), D_MODEL),
        "ffn_w_up": nrm((DEPTH, 2, D_MODEL, D_FF), D_MODEL),
        "ffn_w_down": nrm((DEPTH, 2, D_FF, D_MODEL), D_FF, DEEPNORM_BETA),
        "hyb_w_in": nrm((N_EVEN, D_MODEL, EVEN_IN), D_MODEL),
        "hyb_w_out": nrm((N_EVEN, EVEN_MIX, D_MODEL), EVEN_MIX, DEEPNORM_BETA),
        "gdn_conv_w": nrm((N_EVEN, CONV_K, 3 * B_WIDTH), CONV_K),
        "gdn_a_log": jnp.log(jax.random.uniform(next(ks), (N_EVEN, B_HEADS), jnp.float32, 1.0, 16.0)),
        "gdn_dt_bias": dt + jnp.log(-jnp.expm1(-dt)),
        "gdn_norm_g": gain((N_EVEN, B_HEAD_DIM)),
        "mla_w_in": nrm((N_ODD, D_MODEL, C_IN), D_MODEL),
        "mla_q_norm_g": gain((N_ODD, C_Q_RANK)),
        "mla_w_q_up": nrm((N_ODD, C_Q_RANK, C_HEADS * (C_NOPE + C_ROPE)), C_Q_RANK),
        "mla_kv_norm_g": gain((N_ODD, C_KV_RANK)),
        "mla_w_kv_up": nrm((N_ODD, C_KV_RANK, C_HEADS * (C_NOPE + C_V)), C_KV_RANK),
        "mla_w_out": nrm((N_ODD, C_MIX, D_MODEL), C_MIX, DEEPNORM_BETA),
    }


def reference(x, c, positions, mod_w, mod_b, ln_g, ln_b, ffn_w_gate, ffn_w_up, ffn_w_down,
              hyb_w_in, hyb_w_out, gdn_conv_w, gdn_a_log, gdn_dt_bias, gdn_norm_g,
              mla_w_in, mla_q_norm_g, mla_w_q_up, mla_kv_norm_g, mla_w_kv_up, mla_w_out):
    bsz = x.shape[0]
    cs = jax.nn.silu(c)
    for layer in range(DEPTH):
        mod = (cs @ mod_w[layer] + mod_b[layer]).reshape(bsz, N_SUB, 3, D_MODEL)
        x = adaln_sublayer(x, mod[:, 0], lambda u: 0.5 * swiglu(u, ffn_w_gate[layer, 0], ffn_w_up[layer, 0],
                                                               ffn_w_down[layer, 0]),
                           ln_g[layer, 0], ln_b[layer, 0])
        if layer % 2 == 0:
            e = layer // 2
            x = adaln_sublayer(x, mod[:, 1], lambda u: hybrid_mixer(u, positions, hyb_w_in[e], hyb_w_out[e],
                                                                    gdn_conv_w[e], gdn_a_log[e],
                                                                    gdn_dt_bias[e], gdn_norm_g[e]),
                               ln_g[layer, 1], ln_b[layer, 1])
        else:
            o = layer // 2
            x = adaln_sublayer(x, mod[:, 1], lambda u: mla_mixer(u, positions, mla_w_in[o], mla_q_norm_g[o],
                                                                 mla_w_q_up[o], mla_kv_norm_g[o],
                                                                 mla_w_kv_up[o], mla_w_out[o]),
                               ln_g[layer, 1], ln_b[layer, 1])
        x = adaln_sublayer(x, mod[:, 2], lambda u: 0.5 * swiglu(u, ffn_w_gate[layer, 1], ffn_w_up[layer, 1],
                                                               ffn_w_down[layer, 1]),
                           ln_g[layer, 2], ln_b[layer, 2])
    return x
```

```python
import functools

import numpy as np
import jax
import jax.numpy as jnp
from jax import lax
from jax.experimental import pallas as pl
from jax.experimental.pallas import tpu as pltpu

BF = jnp.bfloat16
F32 = jnp.float32
I32 = jnp.int32

D_MODEL = 1024
DEPTH = 4
CHUNK = 64
DEEPNORM_ALPHA = (2.0 * DEPTH) ** 0.25
D_FF = 2816
N_SUB = 3
LN_EPS = 1e-5
RMS_EPS = 1e-6
NEG_INF = -1e30

A_HEADS = 8
A_HEAD_DIM = 64
A_WIDTH = A_HEADS * A_HEAD_DIM
IDX_HEADS = 8
IDX_DIM = 64
TOPK_MAX = 256

B_HEADS = 4
B_HEAD_DIM = 128
B_WIDTH = B_HEADS * B_HEAD_DIM
CONV_K = 4

EVEN_SIZES = (A_WIDTH, A_WIDTH, A_WIDTH, IDX_HEADS * IDX_DIM, IDX_DIM, IDX_HEADS,
              3 * B_WIDTH, B_HEADS, B_HEADS, B_WIDTH)
EVEN_OFFS = tuple(int(s) for s in np.cumsum((0,) + EVEN_SIZES))

C_HEADS = 16
C_Q_RANK = 384
C_KV_RANK = 256
C_NOPE = 64
C_ROPE = 32
C_V = 64
ROPE_THETA = 10000.0

LANES = 128
ROW_TILE = 512
FF_TILE = 1408
ATT_TILE = 256
INT_MIN = -(2 ** 31)
INT_MAX = 2 ** 31 - 1


def _params(sem, vmem_mb):
    return pltpu.CompilerParams(dimension_semantics=sem, vmem_limit_bytes=vmem_mb << 20)


def _const_spec(shape):
    nd = len(shape)
    return pl.BlockSpec(shape, lambda *_: (0,) * nd, pipeline_mode=pl.Buffered(1))


def _dot(a, b):
    return jnp.dot(a, b, preferred_element_type=F32)


def _dot_nt(a, b):
    return lax.dot_general(a, b, (((1,), (1,)), ((), ())), preferred_element_type=F32)


def _dot_tn(a, b):
    return lax.dot_general(a, b, (((0,), (0,)), ((), ())), preferred_element_type=F32)


def _dot3(a, b):
    a_hi = a.astype(BF)
    b_hi = b.astype(BF)
    a_lo = (a - a_hi.astype(F32)).astype(BF)
    b_lo = (b - b_hi.astype(F32)).astype(BF)
    return _dot(a_hi, b_hi) + (_dot(a_hi, b_lo) + _dot(a_lo, b_hi))


def _silu(x):
    return x * jax.nn.sigmoid(x)


def _layer_norm(z, g, b):
    mu = jnp.mean(z, axis=-1, keepdims=True)
    zc = z - mu
    var = jnp.mean(zc * zc, axis=-1, keepdims=True)
    return zc * lax.rsqrt(var + LN_EPS) * g + b


def _modulate(x, m_ref):
    return x * (1.0 + m_ref[1:2, :]) + m_ref[0:1, :]


def _mod_kernel(c_ref, w_ref, b_ref, o_ref):
    cs = _silu(c_ref[...]).astype(BF)
    o_ref[...] = _dot(cs, w_ref[...].astype(BF)) + b_ref[...]


def _modulation(c, mod_w, mod_b):
    bsz = c.shape[0]
    n = mod_w.shape[-1]
    tn = 1536
    return pl.pallas_call(
        _mod_kernel,
        grid=(DEPTH, n // tn),
        in_specs=[pl.BlockSpec((bsz, D_MODEL), lambda l, j: (0, 0)),
                  pl.BlockSpec((None, D_MODEL, tn), lambda l, j: (l, 0, j)),
                  pl.BlockSpec((None, 1, tn), lambda l, j: (l, 0, j))],
        out_specs=pl.BlockSpec((None, bsz, tn), lambda l, j: (l, 0, j)),
        out_shape=jax.ShapeDtypeStruct((DEPTH, bsz, n), F32),
        compiler_params=_params(("arbitrary", "arbitrary"), 40),
        name="adaln_modulation",
    )(c, mod_w, mod_b.reshape(DEPTH, 1, n))


def _ffn_kernel(x_ref, m_ref, wg_ref, wu_ref, wd_ref, g_ref, b_ref, o_ref, acc_ref):
    x = x_ref[...]
    u = _modulate(x, m_ref).astype(BF)
    for c in range(D_FF // FF_TILE):
        sl = slice(c * FF_TILE, (c + 1) * FF_TILE)
        hg = _dot(u, wg_ref[:, sl])
        hu = _dot(u, wu_ref[:, sl])
        h = (_silu(hg) * hu).astype(BF)
        part = _dot(h, wd_ref[sl, :])
        if c == 0:
            acc_ref[...] = part
        else:
            acc_ref[...] += part
    z = DEEPNORM_ALPHA * x + (1.0 + m_ref[2:3, :]) * (0.5 * acc_ref[...])
    o_ref[...] = _layer_norm(z, g_ref[...], b_ref[...])


def _ffn_sublayer(x, m, wg, wu, wd, ln_g, ln_b):
    bsz, seq, _ = x.shape
    tm = min(ROW_TILE, seq)
    row = pl.BlockSpec((None, tm, D_MODEL), lambda b, i: (b, i, 0))
    return pl.pallas_call(
        _ffn_kernel,
        grid=(bsz, seq // tm),
        in_specs=[row,
                  pl.BlockSpec((None, 3, D_MODEL), lambda b, i: (b, 0, 0)),
                  _const_spec((D_MODEL, D_FF)), _const_spec((D_MODEL, D_FF)),
                  _const_spec((D_FF, D_MODEL)),
                  _const_spec((1, D_MODEL)), _const_spec((1, D_MODEL))],
        out_specs=row,
        out_shape=jax.ShapeDtypeStruct(x.shape, F32),
        scratch_shapes=[pltpu.VMEM((tm, D_MODEL), F32)],
        compiler_params=_params(("parallel", "parallel"), 56),
        name="ffn_sublayer",
    )(x, m, wg.astype(BF), wu.astype(BF), wd.astype(BF), ln_g.reshape(1, -1), ln_b.reshape(1, -1))


def _inproj_kernel(x_ref, m_ref, *refs, n):
    u = _modulate(x_ref[...], m_ref).astype(BF)
    for w_ref, o_ref in zip(refs[:n], refs[n:]):
        o_ref[...] = _dot(u, w_ref[...]).astype(o_ref.dtype)


def _in_projection(x, m, weights, dtypes):
    bsz, seq, _ = x.shape
    tm = min(ROW_TILE, seq)
    n = len(weights)
    row = lambda width: pl.BlockSpec((None, tm, width), lambda b, i: (b, i, 0))
    return pl.pallas_call(
        functools.partial(_inproj_kernel, n=n),
        grid=(bsz, seq // tm),
        in_specs=[row(D_MODEL), pl.BlockSpec((None, 3, D_MODEL), lambda b, i: (b, 0, 0))]
                 + [_const_spec(w.shape) for w in weights],
        out_specs=[row(w.shape[1]) for w in weights],
        out_shape=[jax.ShapeDtypeStruct((bsz, seq, w.shape[1]), dt) for w, dt in zip(weights, dtypes)],
        compiler_params=_params(("parallel", "parallel"), 48),
        name="in_projection",
    )(x, m, *[w.astype(BF) for w in weights])


def _outproj_kernel(x_ref, m_ref, *refs, n):
    g_ref, b_ref, o_ref = refs[2 * n:]
    y = None
    for mix_ref, w_ref in zip(refs[:n], refs[n:2 * n]):
        part = _dot(mix_ref[...].astype(BF), w_ref[...])
        y = part if y is None else y + part
    z = DEEPNORM_ALPHA * x_ref[...] + (1.0 + m_ref[2:3, :]) * y
    o_ref[...] = _layer_norm(z, g_ref[...], b_ref[...])


def _out_projection(x, m, mixes, weights, ln_g, ln_b):
    bsz, seq, _ = x.shape
    tm = min(ROW_TILE, seq)
    n = len(mixes)
    row = lambda width: pl.BlockSpec((None, tm, width), lambda b, i: (b, i, 0))
    return pl.pallas_call(
        functools.partial(_outproj_kernel, n=n),
        grid=(bsz, seq // tm),
        in_specs=[row(D_MODEL), pl.BlockSpec((None, 3, D_MODEL), lambda b, i: (b, 0, 0))]
                 + [row(mx.shape[-1]) for mx in mixes]
                 + [_const_spec(w.shape) for w in weights]
                 + [_const_spec((1, D_MODEL)), _const_spec((1, D_MODEL))],
        out_specs=row(D_MODEL),
        out_shape=jax.ShapeDtypeStruct(x.shape, F32),
        compiler_params=_params(("parallel", "parallel"), 40),
        name="out_projection",
    )(x, m, *mixes, *[w.astype(BF) for w in weights], ln_g.reshape(1, -1), ln_b.reshape(1, -1))


def _rope_table_kernel(pos_ref, invf_ref, cos_ref, sin_ref):
    ang = pos_ref[...].astype(F32) * invf_ref[...]
    lane = lax.broadcasted_iota(I32, ang.shape, 1)
    rot = (lane >= C_NOPE) & (lane < C_NOPE + C_ROPE)
    cos_ref[...] = jnp.where(lane < C_NOPE, 1.0, jnp.where(rot, jnp.cos(ang), 0.0))
    sin_ref[...] = jnp.where(rot, jnp.sin(ang), 0.0)


def _rope_tables(positions):
    bsz, seq = positions.shape
    tm = min(ROW_TILE, seq)
    half = C_ROPE // 2
    inv_freq = ROPE_THETA ** (-jnp.arange(half, dtype=F32) / half)
    invf = jnp.zeros((1, LANES), F32).at[0, C_NOPE:C_NOPE + C_ROPE].set(jnp.tile(inv_freq, 2))
    tab = jax.ShapeDtypeStruct((bsz, seq, LANES), F32)
    return pl.pallas_call(
        _rope_table_kernel,
        grid=(bsz, seq // tm),
        in_specs=[pl.BlockSpec((None, tm, 1), lambda b, i: (b, i, 0)), _const_spec((1, LANES))],
        out_specs=[pl.BlockSpec((None, tm, LANES), lambda b, i: (b, i, 0))] * 2,
        out_shape=[tab, tab],
        compiler_params=_params(("parallel", "parallel"), 32),
        name="rope_tables",
    )(positions.reshape(bsz, seq, 1), invf)


def _rms_norm(x, g):
    return x * lax.rsqrt(jnp.mean(x * x, axis=-1, keepdims=True) + RMS_EPS) * g


def _mla_qkv_kernel(cq_ref, ckv_ref, kr_ref, krot_ref, cos_ref, sin_ref, qg_ref, kvg_ref,
                    wq_ref, wqrot_ref, wk_ref, wv_ref, q_ref, k_ref, v_ref):
    cos = cos_ref[...]
    sin = sin_ref[...]
    cq = _rms_norm(cq_ref[...], qg_ref[...]).astype(BF)
    ckv = _rms_norm(ckv_ref[...], kvg_ref[...]).astype(BF)
    k_rope = kr_ref[...] * cos + krot_ref[...] * sin
    for h in range(C_HEADS):
        sl = slice(h * LANES, (h + 1) * LANES)
        q = _dot(cq, wq_ref[:, sl]) * cos + _dot(cq, wqrot_ref[:, sl]) * sin
        q_ref[:, sl] = q.astype(BF)
        k_ref[:, sl] = (_dot(ckv, wk_ref[:, sl]) + k_rope).astype(BF)
    v_ref[...] = _dot(ckv, wv_ref[...]).astype(BF)


def _mla_qkv(c_q, c_kv, k_r, k_rot, cos, sin, qg, kvg, wq, wqrot, wk, wv):
    bsz, seq, _ = c_q.shape
    tm = min(ROW_TILE, seq)
    row = lambda width: pl.BlockSpec((None, tm, width), lambda b, i: (b, i, 0))
    hw = C_HEADS * LANES
    return pl.pallas_call(
        _mla_qkv_kernel,
        grid=(bsz, seq // tm),
        in_specs=[row(C_Q_RANK), row(C_KV_RANK), row(LANES), row(LANES), row(LANES), row(LANES),
                  _const_spec((1, C_Q_RANK)), _const_spec((1, C_KV_RANK)),
                  _const_spec(wq.shape), _const_spec(wqrot.shape), _const_spec(wk.shape),
                  _const_spec(wv.shape)],
        out_specs=[row(hw), row(hw), row(C_HEADS * C_V)],
        out_shape=[jax.ShapeDtypeStruct((bsz, seq, hw), BF),
                   jax.ShapeDtypeStruct((bsz, seq, hw), BF),
                   jax.ShapeDtypeStruct((bsz, seq, C_HEADS * C_V), BF)],
        compiler_params=_params(("parallel", "parallel"), 40),
        name="mla_qkv",
    )(c_q, c_kv, k_r, k_rot, cos, sin, qg.reshape(1, -1), kvg.reshape(1, -1),
      wq.astype(BF), wqrot.astype(BF), wk.astype(BF), wv.astype(BF))


def _mla_attn_kernel(q_ref, k_ref, v_ref, o_ref, *, scale):
    t = ATT_TILE
    i = pl.program_id(2)
    rq = lax.broadcasted_iota(I32, (t, t), 0)
    ck = lax.broadcasted_iota(I32, (t, t), 1)
    diag_bias = jnp.where((ck >> 6) <= (rq >> 6), 0.0, NEG_INF)
    outs = []
    for hh in range(2):
        hs = slice(hh * LANES, (hh + 1) * LANES)
        q = q_ref[:, hs]

        def step(kt, carry, bias, hs=hs, q=q):
            m, l, acc = carry
            ks = pl.multiple_of(kt * t, t)
            s = _dot_nt(q, k_ref[pl.ds(ks, t), hs]) * scale
            if bias is not None:
                s = s + bias
            m_new = jnp.maximum(m, jnp.max(s, axis=-1, keepdims=True))
            alpha = jnp.exp(m - m_new)
            p = jnp.exp(s - m_new)
            l = alpha * l + jnp.sum(p, axis=-1, keepdims=True)
            acc = alpha * acc + _dot(p.astype(BF), v_ref[pl.ds(ks, t), :])
            return m_new, l, acc

        init = (jnp.full((t, 1), NEG_INF, F32), jnp.zeros((t, 1), F32), jnp.zeros((t, LANES), F32))
        carry = lax.fori_loop(0, i, functools.partial(step, bias=None), init)
        m, l, acc = step(i, carry, diag_bias)
        outs.append(acc / l)
    lane = lax.broadcasted_iota(I32, (t, LANES), 1)
    o_ref[...] = jnp.where(lane < C_V, outs[0], outs[1]).astype(o_ref.dtype)


def _mla_attention(q, k, v):
    bsz, seq, _ = q.shape
    t = ATT_TILE
    return pl.pallas_call(
        functools.partial(_mla_attn_kernel, scale=(C_NOPE + C_ROPE) ** -0.5),
        grid=(bsz, C_HEADS // 2, seq // t),
        in_specs=[pl.BlockSpec((None, t, 2 * LANES), lambda b, h, i: (b, i, h)),
                  pl.BlockSpec((None, seq, 2 * LANES), lambda b, h, i: (b, 0, h)),
                  pl.BlockSpec((None, seq, LANES), lambda b, h, i: (b, 0, h))],
        out_specs=pl.BlockSpec((None, t, LANES), lambda b, h, i: (b, i, h)),
        out_shape=jax.ShapeDtypeStruct((bsz, seq, C_HEADS * C_V), BF),
        compiler_params=_params(("parallel", "parallel", "arbitrary"), 32),
        name="mla_attention",
    )(q, k, v)


def _order_key(s):
    b = lax.bitcast_convert_type(s, I32)
    return b ^ ((b >> 31) & INT_MAX)


def _dsa_kernel(q_ref, k_ref, iq_ref, vt_ref, ik_ref, wt_ref, o_ref, key_ref, bias_ref, thr_ref, *, topk):
    t = ATT_TILE
    c = pl.program_id(1)
    nk = c + 1
    rk = lax.broadcasted_iota(I32, (t, t), 0)
    cq = lax.broadcasted_iota(I32, (t, t), 1)
    vis_diag = (rk >> 6) <= (cq >> 6)
    low_half = lax.broadcasted_iota(I32, (t, LANES), 1) < A_HEAD_DIM

    def head_of_pair(ref, h):
        blk = ref[:, (h // 2) * LANES:(h // 2 + 1) * LANES]
        keep = low_half if h % 2 == 0 else jnp.logical_not(low_half)
        return jnp.where(keep, blk, jnp.zeros_like(blk))

    coef = wt_ref[...] * (IDX_HEADS ** -0.5 * IDX_DIM ** -0.5)
    iq = [head_of_pair(iq_ref, h) for h in range(IDX_HEADS)]

    def score_tile(kt, carry):
        ks = pl.multiple_of(kt * t, t)
        ik = ik_ref[pl.ds(ks, t), :]
        s = jnp.zeros((t, t), F32)
        for h in range(IDX_HEADS):
            s = s + jnp.maximum(_dot_nt(ik, iq[h]), 0.0) * coef[h:h + 1, :]
        s = jnp.where(kt < c, s, jnp.where(vis_diag, s, NEG_INF))
        key_ref[pl.ds(ks, t), :] = _order_key(s)
        return carry

    lax.fori_loop(0, nk, score_tile, 0)

    thr_ref[0:1, :] = jnp.full((1, t), INT_MIN, I32)
    thr_ref[1:2, :] = jnp.full((1, t), INT_MAX, I32)

    def count(pred):
        def body(kt, acc):
            ks = pl.multiple_of(kt * t, t)
            hit = pred(key_ref[pl.ds(ks, t), :], rk + kt * t)
            return acc + jnp.sum(jnp.where(hit, 1, 0).reshape(t // 8, 8, t), axis=0)
        acc = lax.fori_loop(0, nk, body, jnp.zeros((8, t), I32))
        return jnp.sum(acc, axis=0, keepdims=True)

    @pl.when(c > 0)
    def _search():
        n_nonneg = count(lambda keys, idx: keys >= 0)
        thr0 = jnp.where(n_nonneg >= topk, 0, INT_MIN)

        def value_bit(j, thr):
            cand = thr + (jnp.int32(1) << (30 - j))
            n = count(lambda keys, idx: keys >= cand)
            return jnp.where(n >= topk, cand, thr)

        thr = lax.fori_loop(0, 31, value_bit, thr0)
        n_gt = count(lambda keys, idx: keys > thr)
        n_eq = count(lambda keys, idx: keys == thr)
        need = topk - n_gt
        thr_ref[0:1, :] = thr

        @pl.when(jnp.max(n_eq - need) > 0)
        def _ties():
            def index_bit(j, x):
                cand = x + (jnp.int32(1) << (11 - j))
                n = count(lambda keys, idx: (keys == thr) & (idx < cand))
                return jnp.where(n < need, cand, x)
            thr_ref[1:2, :] = lax.fori_loop(0, 12, index_bit, jnp.zeros((1, t), I32))

    thr = thr_ref[0:1, :]
    last = thr_ref[1:2, :]

    def bias_tile(kt, carry):
        ks = pl.multiple_of(kt * t, t)
        keys = key_ref[pl.ds(ks, t), :]
        tie = jnp.where(keys == thr, jnp.where(rk + kt * t <= last, 0.0, NEG_INF), NEG_INF)
        b = jnp.where(keys > thr, 0.0, tie)
        bias_ref[pl.ds(ks, t), :] = jnp.where(kt < c, b, jnp.where(vis_diag, b, NEG_INF))
        return carry

    lax.fori_loop(0, nk, bias_tile, 0)

    rel = (cq - rk).astype(F32)
    scale = A_HEAD_DIM ** -0.5
    for h in range(A_HEADS):
        q = head_of_pair(q_ref, h)
        slope = 2.0 ** (-8.0 * (h + 1) / A_HEADS)
        hp = slice((h // 2) * LANES, (h // 2 + 1) * LANES)
        vrows = slice(h * A_HEAD_DIM, (h + 1) * A_HEAD_DIM)

        def step(kt, carry, q=q, slope=slope, hp=hp, vrows=vrows):
            m, l, acc = carry
            ks = pl.multiple_of(kt * t, t)
            dist = jnp.abs(rel + ((c - kt) * t).astype(F32))
            s = _dot_nt(k_ref[pl.ds(ks, t), hp], q) * scale - slope * dist + bias_ref[pl.ds(ks, t), :]
            m_new = jnp.maximum(m, jnp.max(s, axis=0, keepdims=True))
            alpha = jnp.exp(m - m_new)
            p = jnp.exp(s - m_new)
            l = alpha * l + jnp.sum(p, axis=0, keepdims=True)
            acc = alpha * acc + _dot(vt_ref[vrows, pl.ds(ks, t)], p.astype(BF))
            return m_new, l, acc

        init = (jnp.full((1, t), -1e20, F32), jnp.zeros((1, t), F32), jnp.zeros((A_HEAD_DIM, t), F32))
        m, l, acc = lax.fori_loop(0, nk, step, init)
        o_ref[vrows, :] = (acc / l).astype(o_ref.dtype)


def _dsa_attention(qki, v_t, ik2, w_t, topk):
    bsz, seq, _ = qki.shape
    t = ATT_TILE
    assert topk == t and seq % t == 0 and t == 4 * CHUNK
    return pl.pallas_call(
        functools.partial(_dsa_kernel, topk=topk),
        grid=(bsz, seq // t),
        in_specs=[pl.BlockSpec((None, t, A_WIDTH), lambda b, c: (b, c, 0)),
                  pl.BlockSpec((None, seq, A_WIDTH), lambda b, c: (b, 0, 1)),
                  pl.BlockSpec((None, t, A_WIDTH), lambda b, c: (b, c, 2)),
                  pl.BlockSpec((None, A_WIDTH, seq), lambda b, c: (b, 0, 0)),
                  pl.BlockSpec((None, seq, LANES), lambda b, c: (b, 0, 0)),
                  pl.BlockSpec((None, IDX_HEADS, t), lambda b, c: (b, 0, c))],
        out_specs=pl.BlockSpec((None, A_WIDTH, t), lambda b, c: (b, 0, c)),
        out_shape=jax.ShapeDtypeStruct((bsz, A_WIDTH, seq), BF),
        scratch_shapes=[pltpu.VMEM((seq, t), I32), pltpu.VMEM((seq, t), F32), pltpu.VMEM((8, t), I32)],
        compiler_params=_params(("parallel", "arbitrary"), 40),
        name="dsa_attention",
    )(qki, qki, qki, v_t, ik2, w_t)


def _gdn_kernel(xq_ref, xk_ref, xv_ref, cwq_ref, cwk_ref, cwv_ref, a_ref, b_ref, z_ref,
                alog_ref, dtb_ref, ng_ref, o_ref,
                q_s, k_s, v_s, g_s, beta_s, u_s, w_s, qg_s, kd_s, at_s, egl_s):
    seq = xq_ref.shape[0]
    nchunk = seq // CHUNK
    cs = CHUNK

    row = lax.broadcasted_iota(I32, (seq, LANES), 0)

    def conv_silu(x_ref, cw_ref):
        x = x_ref[...]
        y = x * cw_ref[CONV_K - 1:CONV_K, :]
        for j in range(1, CONV_K):
            shifted = jnp.where(row >= j, pltpu.roll(x, j, 0), 0.0)
            y = y + shifted * cw_ref[CONV_K - 1 - j:CONV_K - j, :]
        return _silu(y)

    def l2n(x):
        return x * lax.rsqrt(jnp.sum(x * x, axis=-1, keepdims=True) + RMS_EPS)

    q_s[...] = l2n(conv_silu(xq_ref, cwq_ref)) * (B_HEAD_DIM ** -0.5)
    k_s[...] = l2n(conv_silu(xk_ref, cwk_ref))
    v_s[...] = conv_silu(xv_ref, cwv_ref)
    pre = a_ref[...] + dtb_ref[...]
    softplus = jnp.maximum(pre, 0.0) + jnp.log(1.0 + jnp.exp(-jnp.abs(pre)))
    g_s[...] = -jnp.exp(alog_ref[...]) * softplus
    beta_s[...] = jax.nn.sigmoid(b_ref[...])

    ri = lax.broadcasted_iota(I32, (cs, cs), 0)
    ci = lax.broadcasted_iota(I32, (cs, cs), 1)
    incl = ri >= ci
    strict = ri > ci
    tril = jnp.where(incl, 1.0, 0.0)
    eye = jnp.where(ri == ci, 1.0, 0.0)
    ones = jnp.ones((cs, cs), F32)

    def local(n, carry):
        r0 = pl.multiple_of(n * cs, cs)
        rows = pl.ds(r0, cs)
        q, k, v = q_s[rows, :], k_s[rows, :], v_s[rows, :]
        beta = beta_s[rows, :]
        gcol = _dot3(tril, jnp.broadcast_to(g_s[rows, :], (cs, LANES)))
        gc = gcol[:, :cs]
        grow = _dot3(ones, jnp.where(ri == ci, gc, 0.0))
        decay = jnp.exp(jnp.where(incl, gc - grow, NEG_INF))
        kb = k * beta
        m = jnp.where(strict, _dot_nt(kb.astype(BF), k.astype(BF)) * decay, 0.0)
        x = -m
        tinv = eye + x
        for _ in range(5):
            x = _dot3(x, x)
            tinv = tinv + _dot3(tinv, x)
        tb = tinv.astype(BF)
        egc = jnp.exp(gcol)
        u_s[rows, :] = _dot(tb, (v * beta).astype(BF))
        w_s[rows, :] = _dot(tb, (kb * egc).astype(BF))
        at_s[rows, :] = jnp.where(incl, _dot_nt(q.astype(BF), k.astype(BF)) * decay, 0.0)
        qg_s[rows, :] = q * egc
        glast = gcol[cs - 1:cs, :]
        kd_s[rows, :] = k * jnp.exp(glast - gcol)
        egl_s[pl.ds(pl.multiple_of(n * 8, 8), 8), :] = jnp.broadcast_to(jnp.exp(glast), (8, LANES))
        return carry

    lax.fori_loop(0, nchunk, local, 0)

    ng = ng_ref[...]

    def scan(n, state):
        r0 = pl.multiple_of(n * cs, cs)
        rows = pl.ds(r0, cs)
        sb = state.astype(BF)
        v_new = u_s[rows, :] - _dot(w_s[rows, :].astype(BF), sb)
        o = _dot(qg_s[rows, :].astype(BF), sb) + _dot(at_s[rows, :].astype(BF), v_new.astype(BF))
        egl = egl_s[pl.ds(pl.multiple_of(n * 8, 8), 1), :]
        state = state * egl + _dot_tn(kd_s[rows, :].astype(BF), v_new.astype(BF))
        on = o * lax.rsqrt(jnp.mean(o * o, axis=-1, keepdims=True) + RMS_EPS) * ng
        o_ref[rows, :] = on * _silu(z_ref[rows, :])
        return state

    lax.fori_loop(0, nchunk, scan, jnp.zeros((B_HEAD_DIM, B_HEAD_DIM), F32))


def _gdn_mixer(b_qkv, a_col, b_col, b_z, conv_w, a_log, dt_bias, norm_g):
    bsz, seq, _ = b_qkv.shape
    col = lambda off: pl.BlockSpec((None, seq, LANES), lambda b, h: (b, 0, h + off))
    cw = lambda off: pl.BlockSpec((CONV_K, LANES), lambda b, h: (0, h + off))
    gate = pl.BlockSpec((None, None, seq, 1), lambda b, h: (b, h, 0, 0))
    scal = pl.BlockSpec((None, 1, 1), lambda b, h: (h, 0, 0))
    big = pltpu.VMEM((seq, LANES), F32)
    thin = pltpu.VMEM((seq, 1), F32)
    return pl.pallas_call(
        _gdn_kernel,
        grid=(bsz, B_HEADS),
        in_specs=[col(0), col(B_HEADS), col(2 * B_HEADS), cw(0), cw(B_HEADS), cw(2 * B_HEADS),
                  gate, gate, col(0), scal, scal, _const_spec((1, B_HEAD_DIM))],
        out_specs=col(0),
        out_shape=jax.ShapeDtypeStruct((bsz, seq, B_WIDTH), F32),
        scratch_shapes=[big, big, big, thin, thin, big, big, big, big,
                        pltpu.VMEM((seq, CHUNK), F32), pltpu.VMEM((seq // CHUNK * 8, LANES), F32)],
        compiler_params=_params(("parallel", "parallel"), 48),
        name="gated_delta_net",
    )(b_qkv, b_qkv, b_qkv, conv_w, conv_w, conv_w, a_col, b_col, b_z,
      a_log.reshape(B_HEADS, 1, 1), dt_bias.reshape(B_HEADS, 1, 1), norm_g.reshape(1, -1))


def _hybrid_sublayer(x, m, positions, w_in, w_out, conv_w, a_log, dt_bias, norm_g, ln_g, ln_b):
    bsz, seq, _ = x.shape
    o = EVEN_OFFS
    cols = lambda i: w_in[:, o[i]:o[i + 1]]
    w_qki = jnp.concatenate([cols(0), cols(1), cols(3)], axis=1)
    w_ik2 = jnp.concatenate([cols(4), cols(4)], axis=1)
    w_small = jnp.concatenate([cols(5), cols(7), cols(8),
                               jnp.zeros((D_MODEL, LANES - IDX_HEADS - 2 * B_HEADS), w_in.dtype)], axis=1)
    qki, a_v, ik2, small, b_qkv, b_z = _in_projection(
        x, m, [w_qki, cols(2), w_ik2, w_small, cols(6), cols(9)], [BF, BF, BF, F32, F32, F32])

    w_t = jnp.swapaxes(small[:, :, :IDX_HEADS], 1, 2)
    out_a_t = _dsa_attention(qki, jnp.swapaxes(a_v, 1, 2), ik2, w_t, min(TOPK_MAX, seq // 4))
    out_a = jnp.swapaxes(out_a_t, 1, 2)

    gates = small[:, :, IDX_HEADS:IDX_HEADS + 2 * B_HEADS]
    a_col = jnp.swapaxes(gates[:, :, :B_HEADS], 1, 2)[..., None]
    b_col = jnp.swapaxes(gates[:, :, B_HEADS:], 1, 2)[..., None]
    out_b = _gdn_mixer(b_qkv, a_col, b_col, b_z, conv_w, a_log, dt_bias, norm_g)

    return _out_projection(x, m, [out_a, out_b], [w_out[:A_WIDTH], w_out[A_WIDTH:]], ln_g, ln_b)


def _rotate_half_cols(w):
    half = C_ROPE // 2
    return jnp.concatenate([-w[..., half:], w[..., :half]], axis=-1)


def _mla_sublayer(x, m, cos, sin, w_in, q_norm_g, w_q_up, kv_norm_g, w_kv_up, w_out, ln_g, ln_b):
    pad = LANES - C_NOPE - C_ROPE
    w_kr = w_in[:, C_Q_RANK + C_KV_RANK:]
    place = lambda w: jnp.pad(w, ((0, 0), (C_NOPE, pad)))
    c_q, c_kv, k_r, k_rot = _in_projection(
        x, m, [w_in[:, :C_Q_RANK], w_in[:, C_Q_RANK:C_Q_RANK + C_KV_RANK], place(w_kr),
               place(_rotate_half_cols(w_kr))], [F32, F32, F32, F32])

    wq = w_q_up.reshape(C_Q_RANK, C_HEADS, C_NOPE + C_ROPE)
    wq_main = jnp.pad(wq, ((0, 0), (0, 0), (0, pad))).reshape(C_Q_RANK, -1)
    wq_rot = jnp.pad(_rotate_half_cols(wq[..., C_NOPE:]), ((0, 0), (0, 0), (C_NOPE, pad))).reshape(C_Q_RANK, -1)
    wkv = w_kv_up.reshape(C_KV_RANK, C_HEADS, C_NOPE + C_V)
    wk = jnp.pad(wkv[..., :C_NOPE], ((0, 0), (0, 0), (0, LANES - C_NOPE))).reshape(C_KV_RANK, -1)
    wv = wkv[..., C_NOPE:].reshape(C_KV_RANK, -1)

    q, k, v = _mla_qkv(c_q, c_kv, k_r, k_rot, cos, sin, q_norm_g, kv_norm_g, wq_main, wq_rot, wk, wv)
    out = _mla_attention(q, k, v)
    return _out_projection(x, m, [out], [w_out], ln_g, ln_b)


def kernel(x, c, positions, mod_w, mod_b, ln_g, ln_b, ffn_w_gate, ffn_w_up, ffn_w_down, hyb_w_in, hyb_w_out, gdn_conv_w, gdn_a_log, gdn_dt_bias, gdn_norm_g, mla_w_in, mla_q_norm_g, mla_w_q_up, mla_kv_norm_g, mla_w_kv_up, mla_w_out):
    bsz = x.shape[0]
    mod = _modulation(c, mod_w, mod_b).reshape(DEPTH, bsz, N_SUB, 3, D_MODEL)
    cos, sin = _rope_tables(positions)
    for layer in range(DEPTH):
        x = _ffn_sublayer(x, mod[layer, :, 0], ffn_w_gate[layer, 0], ffn_w_up[layer, 0], ffn_w_down[layer, 0],
                          ln_g[layer, 0], ln_b[layer, 0])
        if layer % 2 == 0:
            e = layer // 2
            x = _hybrid_sublayer(x, mod[layer, :, 1], positions, hyb_w_in[e], hyb_w_out[e], gdn_conv_w[e],
                                 gdn_a_log[e], gdn_dt_bias[e], gdn_norm_g[e], ln_g[layer, 1], ln_b[layer, 1])
        else:
            o = layer // 2
            x = _mla_sublayer(x, mod[layer, :, 1], cos, sin, mla_w_in[o], mla_q_norm_g[o], mla_w_q_up[o],
                              mla_kv_norm_g[o], mla_w_kv_up[o], mla_w_out[o], ln_g[layer, 1], ln_b[layer, 1])
        x = _ffn_sublayer(x, mod[layer, :, 2], ffn_w_gate[layer, 1], ffn_w_up[layer, 1], ffn_w_down[layer, 1],
                          ln_g[layer, 2], ln_b[layer, 2])
    return x
```

```python
import functools

import numpy as np
import jax
import jax.numpy as jnp
from jax import lax
from jax.experimental import pallas as pl
from jax.experimental.pallas import tpu as pltpu

BF = jnp.bfloat16
F32 = jnp.float32
I32 = jnp.int32

D_MODEL = 1024
DEPTH = 4
CHUNK = 64
DEEPNORM_ALPHA = (2.0 * DEPTH) ** 0.25
D_FF = 2816
N_SUB = 3
LN_EPS = 1e-5
RMS_EPS = 1e-6
NEG_INF = -1e30

A_HEADS = 8
A_HEAD_DIM = 64
A_WIDTH = A_HEADS * A_HEAD_DIM
IDX_HEADS = 8
IDX_DIM = 64
TOPK_MAX = 256

B_HEADS = 4
B_HEAD_DIM = 128
B_WIDTH = B_HEADS * B_HEAD_DIM
CONV_K = 4

EVEN_SIZES = (A_WIDTH, A_WIDTH, A_WIDTH, IDX_HEADS * IDX_DIM, IDX_DIM, IDX_HEADS,
              3 * B_WIDTH, B_HEADS, B_HEADS, B_WIDTH)
EVEN_OFFS = tuple(int(s) for s in np.cumsum((0,) + EVEN_SIZES))

C_HEADS = 16
C_Q_RANK = 384
C_KV_RANK = 256
C_NOPE = 64
C_ROPE = 32
C_V = 64
ROPE_THETA = 10000.0

LANES = 128
ROW_TILE = 512
FF_TILE = 1408
ATT_TILE = 256
INT_MIN = -(2 ** 31)
INT_MAX = 2 ** 31 - 1


def _params(sem, vmem_mb):
    return pltpu.CompilerParams(dimension_semantics=sem, vmem_limit_bytes=vmem_mb << 20)


def _const_spec(shape):
    nd = len(shape)
    return pl.BlockSpec(shape, lambda *_: (0,) * nd, pipeline_mode=pl.Buffered(1))


def _dot(a, b):
    return jnp.dot(a, b, preferred_element_type=F32)


def _dot_nt(a, b):
    return lax.dot_general(a, b, (((1,), (1,)), ((), ())), preferred_element_type=F32)


def _dot_tn(a, b):
    return lax.dot_general(a, b, (((0,), (0,)), ((), ())), preferred_element_type=F32)


def _dot3(a, b):
    a_hi = a.astype(BF)
    b_hi = b.astype(BF)
    a_lo = (a - a_hi.astype(F32)).astype(BF)
    b_lo = (b - b_hi.astype(F32)).astype(BF)
    return _dot(a_hi, b_hi) + (_dot(a_hi, b_lo) + _dot(a_lo, b_hi))


def _silu(x):
    return x * jax.nn.sigmoid(x)


def _layer_norm(z, g, b):
    mu = jnp.mean(z, axis=-1, keepdims=True)
    zc = z - mu
    var = jnp.mean(zc * zc, axis=-1, keepdims=True)
    return zc * lax.rsqrt(var + LN_EPS) * g + b


def _modulate(x, m_ref):
    return x * (1.0 + m_ref[1:2, :]) + m_ref[0:1, :]


def _mod_kernel(c_ref, w_ref, b_ref, o_ref):
    cs = _silu(c_ref[...]).astype(BF)
    o_ref[...] = _dot(cs, w_ref[...].astype(BF)) + b_ref[...]


def _modulation(c, mod_w, mod_b):
    bsz = c.shape[0]
    n = mod_w.shape[-1]
    tn = 1536
    return pl.pallas_call(
        _mod_kernel,
        grid=(DEPTH, n // tn),
        in_specs=[pl.BlockSpec((bsz, D_MODEL), lambda l, j: (0, 0)),
                  pl.BlockSpec((None, D_MODEL, tn), lambda l, j: (l, 0, j)),
                  pl.BlockSpec((None, 1, tn), lambda l, j: (l, 0, j))],
        out_specs=pl.BlockSpec((None, bsz, tn), lambda l, j: (l, 0, j)),
        out_shape=jax.ShapeDtypeStruct((DEPTH, bsz, n), F32),
        compiler_params=_params(("arbitrary", "arbitrary"), 40),
        name="adaln_modulation",
    )(c, mod_w, mod_b.reshape(DEPTH, 1, n))


def _ffn_kernel(x_ref, m_ref, wg_ref, wu_ref, wd_ref, g_ref, b_ref, o_ref, acc_ref):
    x = x_ref[...]
    u = _modulate(x, m_ref).astype(BF)
    for c in range(D_FF // FF_TILE):
        sl = slice(c * FF_TILE, (c + 1) * FF_TILE)
        hg = _dot(u, wg_ref[:, sl])
        hu = _dot(u, wu_ref[:, sl])
        h = (_silu(hg) * hu).astype(BF)
        part = _dot(h, wd_ref[sl, :])
        if c == 0:
            acc_ref[...] = part
        else:
            acc_ref[...] += part
    z = DEEPNORM_ALPHA * x + (1.0 + m_ref[2:3, :]) * (0.5 * acc_ref[...])
    o_ref[...] = _layer_norm(z, g_ref[...], b_ref[...])


def _ffn_sublayer(x, m, wg, wu, wd, ln_g, ln_b):
    bsz, seq, _ = x.shape
    tm = min(ROW_TILE, seq)
    row = pl.BlockSpec((None, tm, D_MODEL), lambda b, i: (b, i, 0))
    return pl.pallas_call(
        _ffn_kernel,
        grid=(bsz, seq // tm),
        in_specs=[row,
                  pl.BlockSpec((None, 3, D_MODEL), lambda b, i: (b, 0, 0)),
                  _const_spec((D_MODEL, D_FF)), _const_spec((D_MODEL, D_FF)),
                  _const_spec((D_FF, D_MODEL)),
                  _const_spec((1, D_MODEL)), _const_spec((1, D_MODEL))],
        out_specs=row,
        out_shape=jax.ShapeDtypeStruct(x.shape, F32),
        scratch_shapes=[pltpu.VMEM((tm, D_MODEL), F32)],
        compiler_params=_params(("parallel", "parallel"), 56),
        name="ffn_sublayer",
    )(x, m, wg.astype(BF), wu.astype(BF), wd.astype(BF), ln_g.reshape(1, -1), ln_b.reshape(1, -1))


def _inproj_kernel(x_ref, m_ref, *refs, n, transposed):
    u = _modulate(x_ref[...], m_ref).astype(BF)
    for w_ref, o_ref, tr in zip(refs[:n], refs[n:], transposed):
        y = _dot_nt(w_ref[...], u) if tr else _dot(u, w_ref[...])
        o_ref[...] = y.astype(o_ref.dtype)


def _in_projection(x, m, weights, dtypes, transposed=None):
    bsz, seq, _ = x.shape
    tm = min(ROW_TILE, seq)
    n = len(weights)
    transposed = transposed or (False,) * n
    row = lambda width: pl.BlockSpec((None, tm, width), lambda b, i: (b, i, 0))
    colb = lambda width: pl.BlockSpec((None, width, tm), lambda b, i: (b, 0, i))
    ws = [(w.T if tr else w).astype(BF) for w, tr in zip(weights, transposed)]
    return pl.pallas_call(
        functools.partial(_inproj_kernel, n=n, transposed=tuple(transposed)),
        grid=(bsz, seq // tm),
        in_specs=[row(D_MODEL), pl.BlockSpec((None, 3, D_MODEL), lambda b, i: (b, 0, 0))]
                 + [_const_spec(w.shape) for w in ws],
        out_specs=[colb(w.shape[1]) if tr else row(w.shape[1]) for w, tr in zip(weights, transposed)],
        out_shape=[jax.ShapeDtypeStruct((bsz, w.shape[1], seq) if tr else (bsz, seq, w.shape[1]), dt)
                   for w, dt, tr in zip(weights, dtypes, transposed)],
        compiler_params=_params(("parallel", "parallel"), 48),
        name="in_projection",
    )(x, m, *ws)


def _outproj_kernel(x_ref, m_ref, *refs, n):
    g_ref, b_ref, o_ref = refs[2 * n:]
    y = None
    for mix_ref, w_ref in zip(refs[:n], refs[n:2 * n]):
        part = _dot(mix_ref[...].astype(BF), w_ref[...])
        y = part if y is None else y + part
    z = DEEPNORM_ALPHA * x_ref[...] + (1.0 + m_ref[2:3, :]) * y
    o_ref[...] = _layer_norm(z, g_ref[...], b_ref[...])


def _out_projection(x, m, mixes, weights, ln_g, ln_b):
    bsz, seq, _ = x.shape
    tm = min(ROW_TILE, seq)
    n = len(mixes)
    row = lambda width: pl.BlockSpec((None, tm, width), lambda b, i: (b, i, 0))
    return pl.pallas_call(
        functools.partial(_outproj_kernel, n=n),
        grid=(bsz, seq // tm),
        in_specs=[row(D_MODEL), pl.BlockSpec((None, 3, D_MODEL), lambda b, i: (b, 0, 0))]
                 + [row(mx.shape[-1]) for mx in mixes]
                 + [_const_spec(w.shape) for w in weights]
                 + [_const_spec((1, D_MODEL)), _const_spec((1, D_MODEL))],
        out_specs=row(D_MODEL),
        out_shape=jax.ShapeDtypeStruct(x.shape, F32),
        compiler_params=_params(("parallel", "parallel"), 40),
        name="out_projection",
    )(x, m, *mixes, *[w.astype(BF) for w in weights], ln_g.reshape(1, -1), ln_b.reshape(1, -1))


def _rope_table_kernel(pos_ref, invf_ref, cos_ref, sin_ref):
    ang = pos_ref[...].astype(F32) * invf_ref[...]
    lane = lax.broadcasted_iota(I32, ang.shape, 1)
    rot = (lane >= C_NOPE) & (lane < C_NOPE + C_ROPE)
    cos_ref[...] = jnp.where(lane < C_NOPE, 1.0, jnp.where(rot, jnp.cos(ang), 0.0))
    sin_ref[...] = jnp.where(rot, jnp.sin(ang), 0.0)


def _rope_tables(positions):
    bsz, seq = positions.shape
    tm = min(ROW_TILE, seq)
    half = C_ROPE // 2
    inv_freq = ROPE_THETA ** (-jnp.arange(half, dtype=F32) / half)
    invf = jnp.zeros((1, LANES), F32).at[0, C_NOPE:C_NOPE + C_ROPE].set(jnp.tile(inv_freq, 2))
    tab = jax.ShapeDtypeStruct((bsz, seq, LANES), F32)
    return pl.pallas_call(
        _rope_table_kernel,
        grid=(bsz, seq // tm),
        in_specs=[pl.BlockSpec((None, tm, 1), lambda b, i: (b, i, 0)), _const_spec((1, LANES))],
        out_specs=[pl.BlockSpec((None, tm, LANES), lambda b, i: (b, i, 0))] * 2,
        out_shape=[tab, tab],
        compiler_params=_params(("parallel", "parallel"), 32),
        name="rope_tables",
    )(positions.reshape(bsz, seq, 1), invf)


def _rms_norm(x, g):
    return x * lax.rsqrt(jnp.mean(x * x, axis=-1, keepdims=True) + RMS_EPS) * g


def _mla_qkv_kernel(cq_ref, ckv_ref, kr_ref, krot_ref, cos_ref, sin_ref, qg_ref, kvg_ref,
                    wq_ref, wqrot_ref, wk_ref, wvt_ref, q_ref, k_ref, vt_ref, *, scale):
    cos = cos_ref[...]
    sin = sin_ref[...]
    cq = _rms_norm(cq_ref[...], qg_ref[...]).astype(BF)
    ckv = _rms_norm(ckv_ref[...], kvg_ref[...]).astype(BF)
    k_rope = kr_ref[...] * cos + krot_ref[...] * sin
    qcos = cos * scale
    qsin = sin * scale
    for h in range(C_HEADS):
        sl = slice(h * LANES, (h + 1) * LANES)
        q = _dot(cq, wq_ref[:, sl]) * qcos + _dot(cq, wqrot_ref[:, sl]) * qsin
        q_ref[:, sl] = q.astype(BF)
        k_ref[:, sl] = (_dot(ckv, wk_ref[:, sl]) + k_rope).astype(BF)
    vt_ref[...] = _dot_nt(wvt_ref[...], ckv).astype(BF)


def _mla_qkv(c_q, c_kv, k_r, k_rot, cos, sin, qg, kvg, wq, wqrot, wk, wv):
    bsz, seq, _ = c_q.shape
    tm = min(ROW_TILE, seq)
    row = lambda width: pl.BlockSpec((None, tm, width), lambda b, i: (b, i, 0))
    hw = C_HEADS * LANES
    vw = C_HEADS * C_V
    return pl.pallas_call(
        functools.partial(_mla_qkv_kernel, scale=(C_NOPE + C_ROPE) ** -0.5),
        grid=(bsz, seq // tm),
        in_specs=[row(C_Q_RANK), row(C_KV_RANK), row(LANES), row(LANES), row(LANES), row(LANES),
                  _const_spec((1, C_Q_RANK)), _const_spec((1, C_KV_RANK)),
                  _const_spec(wq.shape), _const_spec(wqrot.shape), _const_spec(wk.shape),
                  _const_spec((vw, C_KV_RANK))],
        out_specs=[row(hw), row(hw), pl.BlockSpec((None, vw, tm), lambda b, i: (b, 0, i))],
        out_shape=[jax.ShapeDtypeStruct((bsz, seq, hw), BF),
                   jax.ShapeDtypeStruct((bsz, seq, hw), BF),
                   jax.ShapeDtypeStruct((bsz, vw, seq), BF)],
        compiler_params=_params(("parallel", "parallel"), 40),
        name="mla_qkv",
    )(c_q, c_kv, k_r, k_rot, cos, sin, qg.reshape(1, -1), kvg.reshape(1, -1),
      wq.astype(BF), wqrot.astype(BF), wk.astype(BF), wv.T.astype(BF))


def _softmax_heads(logits_fn, values_fn, nheads, dv, m_ref, l_ref, acc_ref):
    ahead = 3
    pending = [logits_fn(h) for h in range(min(ahead, nheads))]
    for h in range(nheads):
        s = pending.pop(0)
        if h + ahead < nheads:
            pending.append(logits_fn(h + ahead))
        rows = slice(h * dv, (h + 1) * dv)
        m_old = m_ref[h]
        m_new = jnp.maximum(m_old, jnp.max(s, axis=0, keepdims=True))
        alpha = jnp.exp(m_old - m_new)
        p = jnp.exp(s - m_new[0:1, :])
        l_ref[h] = alpha * l_ref[h] + jnp.sum(p, axis=0, keepdims=True)
        m_ref[h] = m_new
        acc_ref[rows, :] = alpha[0:1, :] * acc_ref[rows, :] + _dot(values_fn(h), p.astype(BF))


def _softmax_finish(o_ref, l_ref, acc_ref, nheads, dv):
    per = LANES // dv
    for j in range(nheads // per):
        parts = [acc_ref[(per * j + r) * dv:(per * j + r + 1) * dv, :] / l_ref[per * j + r][0:1, :]
                 for r in range(per)]
        o_ref[:, j * LANES:(j + 1) * LANES] = jnp.concatenate(parts, axis=0).T.astype(o_ref.dtype)


MLA_GROUP = 8


def _mla_attn_kernel(q_ref, k_ref, vt_ref, o_ref, m_ref, l_ref, acc_ref):
    t = ATT_TILE
    i = pl.program_id(2)
    rk = lax.broadcasted_iota(I32, (t, t), 0)
    cq = lax.broadcasted_iota(I32, (t, t), 1)
    diag_bias = jnp.where((rk >> 6) <= (cq >> 6), 0.0, NEG_INF)
    m_ref[...] = jnp.full(m_ref.shape, NEG_INF, F32)
    l_ref[...] = jnp.zeros(l_ref.shape, F32)
    acc_ref[...] = jnp.zeros(acc_ref.shape, F32)

    def step(kt, bias):
        ks = pl.multiple_of(kt * t, t)

        def logits(h):
            hs = slice(h * LANES, (h + 1) * LANES)
            s = _dot_nt(k_ref[pl.ds(ks, t), hs], q_ref[:, hs])
            return s if bias is None else s + bias

        _softmax_heads(logits, lambda h: vt_ref[h * C_V:(h + 1) * C_V, pl.ds(ks, t)],
                       MLA_GROUP, C_V, m_ref, l_ref, acc_ref)

    def body(kt, carry):
        step(kt, None)
        return carry

    lax.fori_loop(0, i, body, 0)
    step(i, diag_bias)
    _softmax_finish(o_ref, l_ref, acc_ref, MLA_GROUP, C_V)


def _mla_attention(q, k, v_t):
    bsz, seq, _ = q.shape
    t = ATT_TILE
    g = MLA_GROUP
    return pl.pallas_call(
        _mla_attn_kernel,
        grid=(bsz, C_HEADS // g, seq // t),
        in_specs=[pl.BlockSpec((None, t, g * LANES), lambda b, h, i: (b, i, h)),
                  pl.BlockSpec((None, seq, g * LANES), lambda b, h, i: (b, 0, h)),
                  pl.BlockSpec((None, g * C_V, seq), lambda b, h, i: (b, h, 0))],
        out_specs=pl.BlockSpec((None, t, g * C_V), lambda b, h, i: (b, i, h)),
        out_shape=jax.ShapeDtypeStruct((bsz, seq, C_HEADS * C_V), BF),
        scratch_shapes=[pltpu.VMEM((g, 8, t), F32), pltpu.VMEM((g, 8, t), F32), pltpu.VMEM((g * C_V, t), F32)],
        compiler_params=_params(("parallel", "parallel", "arbitrary"), 40),
        name="mla_attention",
    )(q, k, v_t)


def _order_key(s):
    b = lax.bitcast_convert_type(s, I32)
    return b ^ ((b >> 31) & INT_MAX)


def _dsa_kernel(q_ref, k_ref, iq_ref, vt_ref, ik_ref, wt_ref, o_ref, key_ref, bias_ref, thr_ref,
                m_ref, l_ref, acc_ref, *, topk):
    t = ATT_TILE
    c = pl.program_id(1)
    nk = c + 1
    rk = lax.broadcasted_iota(I32, (t, t), 0)
    cq = lax.broadcasted_iota(I32, (t, t), 1)
    vis_diag = (rk >> 6) <= (cq >> 6)
    low_half = lax.broadcasted_iota(I32, (t, LANES), 1) < A_HEAD_DIM

    def head_of_pair(ref, h):
        blk = ref[:, (h // 2) * LANES:(h // 2 + 1) * LANES]
        keep = low_half if h % 2 == 0 else jnp.logical_not(low_half)
        return jnp.where(keep, blk, jnp.zeros_like(blk))

    coef = wt_ref[...] * (IDX_HEADS ** -0.5 * IDX_DIM ** -0.5)
    iq = [head_of_pair(iq_ref, h) for h in range(IDX_HEADS)]

    def score_tile(kt, carry):
        ks = pl.multiple_of(kt * t, t)
        ik = ik_ref[pl.ds(ks, t), :]
        s = jnp.zeros((t, t), F32)
        for h in range(IDX_HEADS):
            s = s + jnp.maximum(_dot_nt(ik, iq[h]), 0.0) * coef[h:h + 1, :]
        s = jnp.where(kt < c, s, jnp.where(vis_diag, s, NEG_INF))
        key_ref[pl.ds(ks, t), :] = _order_key(s)
        return carry

    lax.fori_loop(0, nk, score_tile, 0)

    thr_ref[0:1, :] = jnp.full((1, t), INT_MIN, I32)
    thr_ref[1:2, :] = jnp.full((1, t), INT_MAX, I32)

    def count(pred):
        def body(kt, acc):
            ks = pl.multiple_of(kt * t, t)
            hit = pred(key_ref[pl.ds(ks, t), :], rk + kt * t)
            return acc + jnp.sum(jnp.where(hit, 1, 0).reshape(t // 8, 8, t), axis=0)
        acc = lax.fori_loop(0, nk, body, jnp.zeros((8, t), I32))
        return jnp.sum(acc, axis=0, keepdims=True)

    @pl.when(c > 0)
    def _search():
        n_nonneg = count(lambda keys, idx: keys >= 0)
        thr0 = jnp.where(n_nonneg >= topk, 0, INT_MIN)

        def value_bit(j, thr):
            cand = thr + (jnp.int32(1) << (30 - j))
            n = count(lambda keys, idx: keys >= cand)
            return jnp.where(n >= topk, cand, thr)

        thr = lax.fori_loop(0, 31, value_bit, thr0)
        n_gt = count(lambda keys, idx: keys > thr)
        n_eq = count(lambda keys, idx: keys == thr)
        need = topk - n_gt
        thr_ref[0:1, :] = thr

        @pl.when(jnp.max(n_eq - need) > 0)
        def _ties():
            def index_bit(j, x):
                cand = x + (jnp.int32(1) << (11 - j))
                n = count(lambda keys, idx: (keys == thr) & (idx < cand))
                return jnp.where(n < need, cand, x)
            thr_ref[1:2, :] = lax.fori_loop(0, 12, index_bit, jnp.zeros((1, t), I32))

    thr = thr_ref[0:1, :]
    last = thr_ref[1:2, :]

    def bias_tile(kt, carry):
        ks = pl.multiple_of(kt * t, t)
        keys = key_ref[pl.ds(ks, t), :]
        tie = jnp.where(keys == thr, jnp.where(rk + kt * t <= last, 0.0, NEG_INF), NEG_INF)
        b = jnp.where(keys > thr, 0.0, tie)
        bias_ref[pl.ds(ks, t), :] = jnp.where(kt < c, b, jnp.where(vis_diag, b, NEG_INF))
        return carry

    lax.fori_loop(0, nk, bias_tile, 0)

    rel = (cq - rk).astype(F32)
    qs = [head_of_pair(q_ref, h) * jnp.asarray(A_HEAD_DIM ** -0.5, BF) for h in range(A_HEADS)]
    m_ref[...] = jnp.full(m_ref.shape, -1e20, F32)
    l_ref[...] = jnp.zeros(l_ref.shape, F32)
    acc_ref[...] = jnp.zeros(acc_ref.shape, F32)

    def attend(kt, carry):
        ks = pl.multiple_of(kt * t, t)
        dist = jnp.abs(rel + ((c - kt) * t).astype(F32))
        bias = bias_ref[pl.ds(ks, t), :]

        def logits(h):
            slope = 2.0 ** (-8.0 * (h + 1) / A_HEADS)
            hp = slice((h // 2) * LANES, (h // 2 + 1) * LANES)
            return _dot_nt(k_ref[pl.ds(ks, t), hp], qs[h]) + (bias - slope * dist)

        _softmax_heads(logits, lambda h: vt_ref[h * A_HEAD_DIM:(h + 1) * A_HEAD_DIM, pl.ds(ks, t)],
                       A_HEADS, A_HEAD_DIM, m_ref, l_ref, acc_ref)
        return carry

    lax.fori_loop(0, nk, attend, 0)
    _softmax_finish(o_ref, l_ref, acc_ref, A_HEADS, A_HEAD_DIM)


def _dsa_attention(qki, v_t, ik2, w_t, topk):
    bsz, seq, _ = qki.shape
    t = ATT_TILE
    assert topk == t and seq % t == 0 and t == 4 * CHUNK
    return pl.pallas_call(
        functools.partial(_dsa_kernel, topk=topk),
        grid=(bsz, seq // t),
        in_specs=[pl.BlockSpec((None, t, A_WIDTH), lambda b, c: (b, c, 0)),
                  pl.BlockSpec((None, seq, A_WIDTH), lambda b, c: (b, 0, 1)),
                  pl.BlockSpec((None, t, A_WIDTH), lambda b, c: (b, c, 2)),
                  pl.BlockSpec((None, A_WIDTH, seq), lambda b, c: (b, 0, 0)),
                  pl.BlockSpec((None, seq, LANES), lambda b, c: (b, 0, 0)),
                  pl.BlockSpec((None, IDX_HEADS, t), lambda b, c: (b, 0, c))],
        out_specs=pl.BlockSpec((None, t, A_WIDTH), lambda b, c: (b, c, 0)),
        out_shape=jax.ShapeDtypeStruct((bsz, seq, A_WIDTH), BF),
        scratch_shapes=[pltpu.VMEM((seq, t), I32), pltpu.VMEM((seq, t), F32), pltpu.VMEM((8, t), I32),
                        pltpu.VMEM((A_HEADS, 8, t), F32), pltpu.VMEM((A_HEADS, 8, t), F32),
                        pltpu.VMEM((A_WIDTH, t), F32)],
        compiler_params=_params(("parallel", "arbitrary"), 40),
        name="dsa_attention",
    )(qki, qki, qki, v_t, ik2, w_t)


GDN_UNROLL = 4


def _gdn_kernel(xq_ref, xk_ref, xv_ref, cwq_ref, cwk_ref, cwv_ref, a_ref, arow_ref, b_ref, z_ref,
                alog_ref, dtb_ref, ng_ref, o_ref,
                q_s, k_s, v_s, gc_s, beta_s, grow_s, u_s, w_s, qg_s, kd_s, at_s, egl_s):
    seq = xq_ref.shape[0]
    nchunk = seq // CHUNK
    cs = CHUNK

    row = lax.broadcasted_iota(I32, (seq, LANES), 0)

    def conv_silu(x_ref, cw_ref):
        x = x_ref[...]
        y = x * cw_ref[CONV_K - 1:CONV_K, :]
        for j in range(1, CONV_K):
            shifted = jnp.where(row >= j, pltpu.roll(x, j, 0), 0.0)
            y = y + shifted * cw_ref[CONV_K - 1 - j:CONV_K - j, :]
        return _silu(y)

    def l2n(x):
        return x * lax.rsqrt(jnp.sum(x * x, axis=-1, keepdims=True) + RMS_EPS)

    q_s[...] = l2n(conv_silu(xq_ref, cwq_ref)) * (B_HEAD_DIM ** -0.5)
    k_s[...] = l2n(conv_silu(xk_ref, cwk_ref))
    v_s[...] = conv_silu(xv_ref, cwv_ref)
    beta_s[...] = jax.nn.sigmoid(b_ref[...])

    def log_decay(a):
        pre = a + dtb_ref[...]
        softplus = jnp.maximum(pre, 0.0) + jnp.log(1.0 + jnp.exp(-jnp.abs(pre)))
        return -jnp.exp(alog_ref[...]) * softplus

    ri = lax.broadcasted_iota(I32, (cs, cs), 0)
    ci = lax.broadcasted_iota(I32, (cs, cs), 1)
    incl = ri >= ci
    strict = ri > ci
    eye = jnp.where(ri == ci, 1.0, 0.0)

    gc = jnp.broadcast_to(log_decay(a_ref[...]), (seq, LANES))
    pos = row & (cs - 1)
    for sh in (1, 2, 4, 8, 16, 32):
        gc = gc + jnp.where(pos >= sh, pltpu.roll(gc, sh, 0), 0.0)
    gc_s[...] = gc
    grow_s[...] = _dot3(log_decay(arow_ref[...]), jnp.where(ri <= ci, 1.0, 0.0))

    def chunk_rows(n):
        return pl.ds(pl.multiple_of(n * cs, cs), cs)

    def local_group(it, carry):
        ns = [it * GDN_UNROLL + j for j in range(GDN_UNROLL)]
        each = lambda f, *lists: [f(*args) for args in zip(*lists)]
        rows = [chunk_rows(n) for n in ns]
        q = [q_s[r, :] for r in rows]
        k = [k_s[r, :] for r in rows]
        beta = [beta_s[r, :] for r in rows]
        gcol = [gc_s[r, :] for r in rows]
        grow = [grow_s[pl.ds(n, 1), :] for n in ns]
        decay = each(lambda gc, gr: jnp.exp(jnp.where(incl, gc[:, :cs] - gr, NEG_INF)), gcol, grow)
        kb = each(lambda a, b: a * b, k, beta)
        kbf = [a.astype(BF) for a in k]
        kk = each(lambda a, b: _dot_nt(a.astype(BF), b), kb, kbf)
        x = each(lambda a, d: -jnp.where(strict, a * d, 0.0), kk, decay)
        tinv = [eye + a for a in x]
        for _ in range(5):
            x = [_dot3(a, a) for a in x]
            tinv = each(lambda tj, a: tj + _dot3(tj, a), tinv, x)
        tb = [a.astype(BF) for a in tinv]
        egc = [jnp.exp(a) for a in gcol]
        vb = each(lambda r, b: (v_s[r, :] * b).astype(BF), rows, beta)
        u = each(_dot, tb, vb)
        w = each(lambda tj, a, e: _dot(tj, (a * e).astype(BF)), tb, kb, egc)
        attn = each(lambda a, b, d: jnp.where(incl, _dot_nt(a.astype(BF), b) * d, 0.0), q, kbf, decay)
        for j, n in enumerate(ns):
            glast = gcol[j][cs - 1:cs, :]
            u_s[rows[j], :] = u[j]
            w_s[rows[j], :] = w[j]
            at_s[rows[j], :] = attn[j]
            qg_s[rows[j], :] = q[j] * egc[j]
            kd_s[rows[j], :] = k[j] * jnp.exp(glast - gcol[j])
            egl_s[pl.ds(pl.multiple_of(n * 8, 8), 8), :] = jnp.broadcast_to(jnp.exp(glast), (8, LANES))
        return carry

    lax.fori_loop(0, nchunk // GDN_UNROLL, local_group, 0)

    ng = ng_ref[...]

    def scan(n, state):
        r0 = pl.multiple_of(n * cs, cs)
        rows = pl.ds(r0, cs)
        sb = state.astype(BF)
        v_new = u_s[rows, :] - _dot(w_s[rows, :].astype(BF), sb)
        o = _dot(qg_s[rows, :].astype(BF), sb) + _dot(at_s[rows, :].astype(BF), v_new.astype(BF))
        egl = egl_s[pl.ds(pl.multiple_of(n * 8, 8), 1), :]
        state = state * egl + _dot_tn(kd_s[rows, :].astype(BF), v_new.astype(BF))
        on = o * lax.rsqrt(jnp.mean(o * o, axis=-1, keepdims=True) + RMS_EPS) * ng
        o_ref[rows, :] = on * _silu(z_ref[rows, :])
        return state

    lax.fori_loop(0, nchunk, scan, jnp.zeros((B_HEAD_DIM, B_HEAD_DIM), F32))


def _gdn_mixer(b_qkv, a_col, a_rows, b_col, b_z, conv_w, a_log, dt_bias, norm_g):
    bsz, seq, _ = b_qkv.shape
    nchunk = seq // CHUNK
    assert nchunk % GDN_UNROLL == 0
    col = lambda off: pl.BlockSpec((None, seq, LANES), lambda b, h: (b, 0, h + off))
    cw = lambda off: pl.BlockSpec((CONV_K, LANES), lambda b, h: (0, h + off))
    gate = pl.BlockSpec((None, None, seq, 1), lambda b, h: (b, h, 0, 0))
    gate_rows = pl.BlockSpec((None, None, nchunk, CHUNK), lambda b, h: (b, h, 0, 0))
    scal = pl.BlockSpec((None, 1, 1), lambda b, h: (h, 0, 0))
    big = pltpu.VMEM((seq, LANES), F32)
    return pl.pallas_call(
        _gdn_kernel,
        grid=(bsz, B_HEADS),
        in_specs=[col(0), col(B_HEADS), col(2 * B_HEADS), cw(0), cw(B_HEADS), cw(2 * B_HEADS),
                  gate, gate_rows, gate, col(0), scal, scal, _const_spec((1, B_HEAD_DIM))],
        out_specs=col(0),
        out_shape=jax.ShapeDtypeStruct((bsz, seq, B_WIDTH), F32),
        scratch_shapes=[big, big, big, big, pltpu.VMEM((seq, 1), F32), pltpu.VMEM((nchunk, CHUNK), F32),
                        big, big, big, big,
                        pltpu.VMEM((seq, CHUNK), F32), pltpu.VMEM((nchunk * 8, LANES), F32)],
        compiler_params=_params(("parallel", "parallel"), 48),
        name="gated_delta_net",
    )(b_qkv, b_qkv, b_qkv, conv_w, conv_w, conv_w, a_col, a_rows, b_col, b_z,
      a_log.reshape(B_HEADS, 1, 1), dt_bias.reshape(B_HEADS, 1, 1), norm_g.reshape(1, -1))


def _hybrid_sublayer(x, m, positions, w_in, w_out, conv_w, a_log, dt_bias, norm_g, ln_g, ln_b):
    bsz, seq, _ = x.shape
    o = EVEN_OFFS
    cols = lambda i: w_in[:, o[i]:o[i + 1]]
    w_qki = jnp.concatenate([cols(0), cols(1), cols(3)], axis=1)
    w_ik2 = jnp.concatenate([cols(4), cols(4)], axis=1)
    w_small = jnp.concatenate([cols(5), cols(7), cols(8),
                               jnp.zeros((D_MODEL, LANES - IDX_HEADS - 2 * B_HEADS), w_in.dtype)], axis=1)
    qki, a_vt, ik2, small_t, b_qkv, b_z = _in_projection(
        x, m, [w_qki, cols(2), w_ik2, w_small, cols(6), cols(9)], [BF, BF, BF, F32, F32, F32],
        transposed=(False, True, False, True, False, False))

    out_a = _dsa_attention(qki, a_vt, ik2, small_t[:, :IDX_HEADS], min(TOPK_MAX, seq // 4))

    a_t = small_t[:, IDX_HEADS:IDX_HEADS + B_HEADS]
    b_t = small_t[:, IDX_HEADS + B_HEADS:IDX_HEADS + 2 * B_HEADS]
    out_b = _gdn_mixer(b_qkv, a_t[..., None], a_t.reshape(bsz, B_HEADS, seq // CHUNK, CHUNK),
                       b_t[..., None], b_z, conv_w, a_log, dt_bias, norm_g)

    return _out_projection(x, m, [out_a, out_b], [w_out[:A_WIDTH], w_out[A_WIDTH:]], ln_g, ln_b)


def _rotate_half_cols(w):
    half = C_ROPE // 2
    return jnp.concatenate([-w[..., half:], w[..., :half]], axis=-1)


def _mla_sublayer(x, m, cos, sin, w_in, q_norm_g, w_q_up, kv_norm_g, w_kv_up, w_out, ln_g, ln_b):
    pad = LANES - C_NOPE - C_ROPE
    w_kr = w_in[:, C_Q_RANK + C_KV_RANK:]
    place = lambda w: jnp.pad(w, ((0, 0), (C_NOPE, pad)))
    c_q, c_kv, k_r, k_rot = _in_projection(
        x, m, [w_in[:, :C_Q_RANK], w_in[:, C_Q_RANK:C_Q_RANK + C_KV_RANK], place(w_kr),
               place(_rotate_half_cols(w_kr))], [F32, F32, F32, F32])

    wq = w_q_up.reshape(C_Q_RANK, C_HEADS, C_NOPE + C_ROPE)
    wq_main = jnp.pad(wq, ((0, 0), (0, 0), (0, pad))).reshape(C_Q_RANK, -1)
    wq_rot = jnp.pad(_rotate_half_cols(wq[..., C_NOPE:]), ((0, 0), (0, 0), (C_NOPE, pad))).reshape(C_Q_RANK, -1)
    wkv = w_kv_up.reshape(C_KV_RANK, C_HEADS, C_NOPE + C_V)
    wk = jnp.pad(wkv[..., :C_NOPE], ((0, 0), (0, 0), (0, LANES - C_NOPE))).reshape(C_KV_RANK, -1)
    wv = wkv[..., C_NOPE:].reshape(C_KV_RANK, -1)

    q, k, v = _mla_qkv(c_q, c_kv, k_r, k_rot, cos, sin, q_norm_g, kv_norm_g, wq_main, wq_rot, wk, wv)
    out = _mla_attention(q, k, v)
    return _out_projection(x, m, [out], [w_out], ln_g, ln_b)


def kernel(x, c, positions, mod_w, mod_b, ln_g, ln_b, ffn_w_gate, ffn_w_up, ffn_w_down, hyb_w_in, hyb_w_out, gdn_conv_w, gdn_a_log, gdn_dt_bias, gdn_norm_g, mla_w_in, mla_q_norm_g, mla_w_q_up, mla_kv_norm_g, mla_w_kv_up, mla_w_out):
    bsz = x.shape[0]
    mod = _modulation(c, mod_w, mod_b).reshape(DEPTH, bsz, N_SUB, 3, D_MODEL)
    cos, sin = _rope_tables(positions)
    for layer in range(DEPTH):
        x = _ffn_sublayer(x, mod[layer, :, 0], ffn_w_gate[layer, 0], ffn_w_up[layer, 0], ffn_w_down[layer, 0],
                          ln_g[layer, 0], ln_b[layer, 0])
        if layer % 2 == 0:
            e = layer // 2
            x = _hybrid_sublayer(x, mod[layer, :, 1], positions, hyb_w_in[e], hyb_w_out[e], gdn_conv_w[e],
                                 gdn_a_log[e], gdn_dt_bias[e], gdn_norm_g[e], ln_g[layer, 1], ln_b[layer, 1])
        else:
            o = layer // 2
            x = _mla_sublayer(x, mod[layer, :, 1], cos, sin, mla_w_in[o], mla_q_norm_g[o], mla_w_q_up[o],
                              mla_kv_norm_g[o], mla_w_kv_up[o], mla_w_out[o], ln_g[layer, 1], ln_b[layer, 1])
        x = _ffn_sublayer(x, mod[layer, :, 2], ffn_w_gate[layer, 1], ffn_w_up[layer, 1], ffn_w_down[layer, 1],
                          ln_g[layer, 2], ln_b[layer, 2])
    return x
```

```python
import functools

import numpy as np
import jax
import jax.numpy as jnp
from jax import lax
from jax.experimental import pallas as pl
from jax.experimental.pallas import tpu as pltpu

BF = jnp.bfloat16
F32 = jnp.float32
I32 = jnp.int32

D_MODEL = 1024
DEPTH = 4
CHUNK = 64
DEEPNORM_ALPHA = (2.0 * DEPTH) ** 0.25
D_FF = 2816
N_SUB = 3
LN_EPS = 1e-5
RMS_EPS = 1e-6
NEG_INF = -1e30

A_HEADS = 8
A_HEAD_DIM = 64
A_WIDTH = A_HEADS * A_HEAD_DIM
IDX_HEADS = 8
IDX_DIM = 64
TOPK_MAX = 256

B_HEADS = 4
B_HEAD_DIM = 128
B_WIDTH = B_HEADS * B_HEAD_DIM
CONV_K = 4

EVEN_SIZES = (A_WIDTH, A_WIDTH, A_WIDTH, IDX_HEADS * IDX_DIM, IDX_DIM, IDX_HEADS,
              3 * B_WIDTH, B_HEADS, B_HEADS, B_WIDTH)
EVEN_OFFS = tuple(int(s) for s in np.cumsum((0,) + EVEN_SIZES))

C_HEADS = 16
C_Q_RANK = 384
C_KV_RANK = 256
C_NOPE = 64
C_ROPE = 32
C_V = 64
ROPE_THETA = 10000.0

LANES = 128
ROW_TILE = 512
FF_TILE = 1408
ATT_TILE = 256
LOG2E = 1.4426950408889634
INT_MIN = -(2 ** 31)
INT_MAX = 2 ** 31 - 1


def _params(sem, vmem_mb):
    return pltpu.CompilerParams(dimension_semantics=sem, vmem_limit_bytes=vmem_mb << 20)


def _const_spec(shape):
    nd = len(shape)
    return pl.BlockSpec(shape, lambda *_: (0,) * nd, pipeline_mode=pl.Buffered(1))


def _dot(a, b):
    return jnp.dot(a, b, preferred_element_type=F32)


def _dot_nt(a, b):
    return lax.dot_general(a, b, (((1,), (1,)), ((), ())), preferred_element_type=F32)


def _dot_tn(a, b):
    return lax.dot_general(a, b, (((0,), (0,)), ((), ())), preferred_element_type=F32)


def _dot3(a, b):
    a_hi = a.astype(BF)
    b_hi = b.astype(BF)
    a_lo = (a - a_hi.astype(F32)).astype(BF)
    b_lo = (b - b_hi.astype(F32)).astype(BF)
    return _dot(a_hi, b_hi) + (_dot(a_hi, b_lo) + _dot(a_lo, b_hi))


def _silu(x):
    return x * jax.nn.sigmoid(x)


def _layer_norm(z, g, b):
    mu = jnp.mean(z, axis=-1, keepdims=True)
    zc = z - mu
    var = jnp.mean(zc * zc, axis=-1, keepdims=True)
    return zc * lax.rsqrt(var + LN_EPS) * g + b


def _modulate(x, m_ref):
    return x * (1.0 + m_ref[1:2, :]) + m_ref[0:1, :]


def _mod_kernel(c_ref, w_ref, b_ref, o_ref):
    cs = _silu(c_ref[...]).astype(BF)
    o_ref[...] = _dot(cs, w_ref[...].astype(BF)) + b_ref[...]


def _modulation(c, mod_w, mod_b):
    bsz = c.shape[0]
    n = mod_w.shape[-1]
    tn = 1536
    return pl.pallas_call(
        _mod_kernel,
        grid=(DEPTH, n // tn),
        in_specs=[pl.BlockSpec((bsz, D_MODEL), lambda l, j: (0, 0)),
                  pl.BlockSpec((None, D_MODEL, tn), lambda l, j: (l, 0, j)),
                  pl.BlockSpec((None, 1, tn), lambda l, j: (l, 0, j))],
        out_specs=pl.BlockSpec((None, bsz, tn), lambda l, j: (l, 0, j)),
        out_shape=jax.ShapeDtypeStruct((DEPTH, bsz, n), F32),
        compiler_params=_params(("arbitrary", "arbitrary"), 40),
        name="adaln_modulation",
    )(c, mod_w, mod_b.reshape(DEPTH, 1, n))


def _ffn_kernel(x_ref, m_ref, wg_ref, wu_ref, wd_ref, g_ref, b_ref, o_ref, acc_ref):
    x = x_ref[...]
    u = _modulate(x, m_ref).astype(BF)
    for c in range(D_FF // FF_TILE):
        sl = slice(c * FF_TILE, (c + 1) * FF_TILE)
        hg = _dot(u, wg_ref[:, sl])
        hu = _dot(u, wu_ref[:, sl])
        h = (_silu(hg) * hu).astype(BF)
        part = _dot(h, wd_ref[sl, :])
        if c == 0:
            acc_ref[...] = part
        else:
            acc_ref[...] += part
    z = DEEPNORM_ALPHA * x + (1.0 + m_ref[2:3, :]) * (0.5 * acc_ref[...])
    o_ref[...] = _layer_norm(z, g_ref[...], b_ref[...])


def _ffn_sublayer(x, m, wg, wu, wd, ln_g, ln_b):
    bsz, seq, _ = x.shape
    tm = min(ROW_TILE, seq)
    row = pl.BlockSpec((None, tm, D_MODEL), lambda b, i: (b, i, 0))
    return pl.pallas_call(
        _ffn_kernel,
        grid=(bsz, seq // tm),
        in_specs=[row,
                  pl.BlockSpec((None, 3, D_MODEL), lambda b, i: (b, 0, 0)),
                  _const_spec((D_MODEL, D_FF)), _const_spec((D_MODEL, D_FF)),
                  _const_spec((D_FF, D_MODEL)),
                  _const_spec((1, D_MODEL)), _const_spec((1, D_MODEL))],
        out_specs=row,
        out_shape=jax.ShapeDtypeStruct(x.shape, F32),
        scratch_shapes=[pltpu.VMEM((tm, D_MODEL), F32)],
        compiler_params=_params(("parallel", "parallel"), 56),
        name="ffn_sublayer",
    )(x, m, wg.astype(BF), wu.astype(BF), wd.astype(BF), ln_g.reshape(1, -1), ln_b.reshape(1, -1))


def _inproj_kernel(x_ref, m_ref, s_ref, *refs, n, transposed):
    u = _modulate(x_ref[...], m_ref).astype(BF)
    for g, (w_ref, o_ref, tr) in enumerate(zip(refs[:n], refs[n:], transposed)):
        y = _dot_nt(w_ref[...], u) if tr else _dot(u, w_ref[...])
        if g == 0:
            y = y * s_ref[...]
        o_ref[...] = y.astype(o_ref.dtype)


def _in_projection(x, m, weights, dtypes, transposed=None, scale0=None):
    bsz, seq, _ = x.shape
    tm = min(ROW_TILE, seq)
    n = len(weights)
    transposed = transposed or (False,) * n
    assert not transposed[0]
    n0 = weights[0].shape[1]
    scale0 = jnp.ones((n0,), F32) if scale0 is None else scale0
    row = lambda width: pl.BlockSpec((None, tm, width), lambda b, i: (b, i, 0))
    colb = lambda width: pl.BlockSpec((None, width, tm), lambda b, i: (b, 0, i))
    ws = [(w.T if tr else w).astype(BF) for w, tr in zip(weights, transposed)]
    return pl.pallas_call(
        functools.partial(_inproj_kernel, n=n, transposed=tuple(transposed)),
        grid=(bsz, seq // tm),
        in_specs=[row(D_MODEL), pl.BlockSpec((None, 3, D_MODEL), lambda b, i: (b, 0, 0)),
                  _const_spec((1, n0))] + [_const_spec(w.shape) for w in ws],
        out_specs=[colb(w.shape[1]) if tr else row(w.shape[1]) for w, tr in zip(weights, transposed)],
        out_shape=[jax.ShapeDtypeStruct((bsz, w.shape[1], seq) if tr else (bsz, seq, w.shape[1]), dt)
                   for w, dt, tr in zip(weights, dtypes, transposed)],
        compiler_params=_params(("parallel", "parallel"), 48),
        name="in_projection",
    )(x, m, scale0.reshape(1, n0), *ws)


def _outproj_kernel(x_ref, m_ref, *refs, n):
    g_ref, b_ref, o_ref = refs[2 * n:]
    y = None
    for mix_ref, w_ref in zip(refs[:n], refs[n:2 * n]):
        part = _dot(mix_ref[...].astype(BF), w_ref[...])
        y = part if y is None else y + part
    z = DEEPNORM_ALPHA * x_ref[...] + (1.0 + m_ref[2:3, :]) * y
    o_ref[...] = _layer_norm(z, g_ref[...], b_ref[...])


def _out_projection(x, m, mixes, weights, ln_g, ln_b):
    bsz, seq, _ = x.shape
    tm = min(ROW_TILE, seq)
    n = len(mixes)
    row = lambda width: pl.BlockSpec((None, tm, width), lambda b, i: (b, i, 0))
    return pl.pallas_call(
        functools.partial(_outproj_kernel, n=n),
        grid=(bsz, seq // tm),
        in_specs=[row(D_MODEL), pl.BlockSpec((None, 3, D_MODEL), lambda b, i: (b, 0, 0))]
                 + [row(mx.shape[-1]) for mx in mixes]
                 + [_const_spec(w.shape) for w in weights]
                 + [_const_spec((1, D_MODEL)), _const_spec((1, D_MODEL))],
        out_specs=row(D_MODEL),
        out_shape=jax.ShapeDtypeStruct(x.shape, F32),
        compiler_params=_params(("parallel", "parallel"), 40),
        name="out_projection",
    )(x, m, *mixes, *[w.astype(BF) for w in weights], ln_g.reshape(1, -1), ln_b.reshape(1, -1))


def _rope_table_kernel(pos_ref, invf_ref, cos_ref, sin_ref):
    ang = pos_ref[...].astype(F32) * invf_ref[...]
    lane = lax.broadcasted_iota(I32, ang.shape, 1)
    rot = (lane >= C_NOPE) & (lane < C_NOPE + C_ROPE)
    cos_ref[...] = jnp.where(lane < C_NOPE, 1.0, jnp.where(rot, jnp.cos(ang), 0.0))
    sin_ref[...] = jnp.where(rot, jnp.sin(ang), 0.0)


def _rope_tables(positions):
    bsz, seq = positions.shape
    tm = min(ROW_TILE, seq)
    half = C_ROPE // 2
    inv_freq = ROPE_THETA ** (-jnp.arange(half, dtype=F32) / half)
    invf = jnp.zeros((1, LANES), F32).at[0, C_NOPE:C_NOPE + C_ROPE].set(jnp.tile(inv_freq, 2))
    tab = jax.ShapeDtypeStruct((bsz, seq, LANES), F32)
    return pl.pallas_call(
        _rope_table_kernel,
        grid=(bsz, seq // tm),
        in_specs=[pl.BlockSpec((None, tm, 1), lambda b, i: (b, i, 0)), _const_spec((1, LANES))],
        out_specs=[pl.BlockSpec((None, tm, LANES), lambda b, i: (b, i, 0))] * 2,
        out_shape=[tab, tab],
        compiler_params=_params(("parallel", "parallel"), 32),
        name="rope_tables",
    )(positions.reshape(bsz, seq, 1), invf)


def _rms_norm(x, g):
    return x * lax.rsqrt(jnp.mean(x * x, axis=-1, keepdims=True) + RMS_EPS) * g


def _mla_qkv_kernel(cq_ref, ckv_ref, kr_ref, krot_ref, cos_ref, sin_ref, qg_ref, kvg_ref,
                    wq_ref, wqrot_ref, wk_ref, wvt_ref, q_ref, k_ref, vt_ref, *, scale):
    cos = cos_ref[...]
    sin = sin_ref[...]
    cq = _rms_norm(cq_ref[...], qg_ref[...]).astype(BF)
    ckv = _rms_norm(ckv_ref[...], kvg_ref[...]).astype(BF)
    k_rope = kr_ref[...] * cos + krot_ref[...] * sin
    qcos = cos * scale
    qsin = sin * scale
    for h in range(C_HEADS):
        sl = slice(h * LANES, (h + 1) * LANES)
        q = _dot(cq, wq_ref[:, sl]) * qcos + _dot(cq, wqrot_ref[:, sl]) * qsin
        q_ref[:, sl] = q.astype(BF)
        k_ref[:, sl] = (_dot(ckv, wk_ref[:, sl]) + k_rope).astype(BF)
    vt_ref[...] = _dot_nt(wvt_ref[...], ckv).astype(BF)


def _mla_qkv(c_q, c_kv, k_r, k_rot, cos, sin, qg, kvg, wq, wqrot, wk, wv):
    bsz, seq, _ = c_q.shape
    tm = min(ROW_TILE, seq)
    row = lambda width: pl.BlockSpec((None, tm, width), lambda b, i: (b, i, 0))
    hw = C_HEADS * LANES
    vw = C_HEADS * C_V
    return pl.pallas_call(
        functools.partial(_mla_qkv_kernel, scale=(C_NOPE + C_ROPE) ** -0.5 * LOG2E),
        grid=(bsz, seq // tm),
        in_specs=[row(C_Q_RANK), row(C_KV_RANK), row(LANES), row(LANES), row(LANES), row(LANES),
                  _const_spec((1, C_Q_RANK)), _const_spec((1, C_KV_RANK)),
                  _const_spec(wq.shape), _const_spec(wqrot.shape), _const_spec(wk.shape),
                  _const_spec((vw, C_KV_RANK))],
        out_specs=[row(hw), row(hw), pl.BlockSpec((None, vw, tm), lambda b, i: (b, 0, i))],
        out_shape=[jax.ShapeDtypeStruct((bsz, seq, hw), BF),
                   jax.ShapeDtypeStruct((bsz, seq, hw), BF),
                   jax.ShapeDtypeStruct((bsz, vw, seq), BF)],
        compiler_params=_params(("parallel", "parallel"), 40),
        name="mla_qkv",
    )(c_q, c_kv, k_r, k_rot, cos, sin, qg.reshape(1, -1), kvg.reshape(1, -1),
      wq.astype(BF), wqrot.astype(BF), wk.astype(BF), wv.T.astype(BF))


def _softmax_heads(logits_fn, values_fn, nheads, dv, m_ref, l_ref, acc_ref):
    ahead = 3
    pending = [logits_fn(h) for h in range(min(ahead, nheads))]
    for h in range(nheads):
        s = pending.pop(0)
        if h + ahead < nheads:
            pending.append(logits_fn(h + ahead))
        rows = slice(h * dv, (h + 1) * dv)
        m_old = m_ref[h]
        m_new = jnp.maximum(m_old, jnp.max(s, axis=0, keepdims=True))
        alpha = jnp.exp2(m_old - m_new)
        p = jnp.exp2(s - m_new[0:1, :])
        l_ref[h] = alpha * l_ref[h] + jnp.sum(p, axis=0, keepdims=True)
        m_ref[h] = m_new
        acc_ref[rows, :] = alpha[0:1, :] * acc_ref[rows, :] + _dot(values_fn(h), p.astype(BF))


def _softmax_finish(o_ref, l_ref, acc_ref, nheads, dv):
    per = LANES // dv
    for j in range(nheads // per):
        parts = [acc_ref[(per * j + r) * dv:(per * j + r + 1) * dv, :] / l_ref[per * j + r][0:1, :]
                 for r in range(per)]
        o_ref[:, j * LANES:(j + 1) * LANES] = jnp.concatenate(parts, axis=0).T.astype(o_ref.dtype)


MLA_GROUP = 8


def _mla_attn_kernel(q_ref, k_ref, vt_ref, o_ref, m_ref, l_ref, acc_ref):
    t = ATT_TILE
    i = pl.program_id(2)
    rk = lax.broadcasted_iota(I32, (t, t), 0)
    cq = lax.broadcasted_iota(I32, (t, t), 1)
    diag_bias = jnp.where((rk >> 6) <= (cq >> 6), 0.0, NEG_INF)
    m_ref[...] = jnp.full(m_ref.shape, NEG_INF, F32)
    l_ref[...] = jnp.zeros(l_ref.shape, F32)
    acc_ref[...] = jnp.zeros(acc_ref.shape, F32)

    def step(kt, bias):
        ks = pl.multiple_of(kt * t, t)

        def logits(h):
            hs = slice(h * LANES, (h + 1) * LANES)
            s = _dot_nt(k_ref[pl.ds(ks, t), hs], q_ref[:, hs])
            return s if bias is None else s + bias

        _softmax_heads(logits, lambda h: vt_ref[h * C_V:(h + 1) * C_V, pl.ds(ks, t)],
                       MLA_GROUP, C_V, m_ref, l_ref, acc_ref)

    def body(kt, carry):
        step(kt, None)
        return carry

    lax.fori_loop(0, i, body, 0)
    step(i, diag_bias)
    _softmax_finish(o_ref, l_ref, acc_ref, MLA_GROUP, C_V)


def _mla_attention(q, k, v_t):
    bsz, seq, _ = q.shape
    t = ATT_TILE
    g = MLA_GROUP
    return pl.pallas_call(
        _mla_attn_kernel,
        grid=(bsz, C_HEADS // g, seq // t),
        in_specs=[pl.BlockSpec((None, t, g * LANES), lambda b, h, i: (b, i, h)),
                  pl.BlockSpec((None, seq, g * LANES), lambda b, h, i: (b, 0, h)),
                  pl.BlockSpec((None, g * C_V, seq), lambda b, h, i: (b, h, 0))],
        out_specs=pl.BlockSpec((None, t, g * C_V), lambda b, h, i: (b, i, h)),
        out_shape=jax.ShapeDtypeStruct((bsz, seq, C_HEADS * C_V), BF),
        scratch_shapes=[pltpu.VMEM((g, 8, t), F32), pltpu.VMEM((g, 8, t), F32), pltpu.VMEM((g * C_V, t), F32)],
        compiler_params=_params(("parallel", "parallel", "arbitrary"), 40),
        name="mla_attention",
    )(q, k, v_t)


def _order_key(s):
    b = lax.bitcast_convert_type(s, I32)
    return b ^ ((b >> 31) & INT_MAX)


def _dsa_kernel(q_ref, k_ref, iq_ref, vt_ref, ik_ref, wt_ref, o_ref, key_ref, bias_ref, thr_ref,
                m_ref, l_ref, acc_ref, *, topk):
    t = ATT_TILE
    c = pl.program_id(1)
    nk = c + 1
    rk = lax.broadcasted_iota(I32, (t, t), 0)
    cq = lax.broadcasted_iota(I32, (t, t), 1)
    vis_diag = (rk >> 6) <= (cq >> 6)
    low_half = lax.broadcasted_iota(I32, (t, LANES), 1) < A_HEAD_DIM

    def head_of_pair(ref, h):
        blk = ref[:, (h // 2) * LANES:(h // 2 + 1) * LANES]
        keep = low_half if h % 2 == 0 else jnp.logical_not(low_half)
        return jnp.where(keep, blk, jnp.zeros_like(blk))

    coef = wt_ref[...] * (IDX_HEADS ** -0.5 * IDX_DIM ** -0.5)
    iq = [head_of_pair(iq_ref, h) for h in range(IDX_HEADS)]

    def score_tile(kt, carry):
        ks = pl.multiple_of(kt * t, t)
        ik = ik_ref[pl.ds(ks, t), :]
        s = jnp.zeros((t, t), F32)
        for h in range(IDX_HEADS):
            s = s + jnp.maximum(_dot_nt(ik, iq[h]), 0.0) * coef[h:h + 1, :]
        s = jnp.where(kt < c, s, jnp.where(vis_diag, s, NEG_INF))
        key_ref[pl.ds(ks, t), :] = _order_key(s)
        return carry

    lax.fori_loop(0, nk, score_tile, 0)

    thr_ref[0:1, :] = jnp.full((1, t), INT_MIN, I32)
    thr_ref[1:2, :] = jnp.full((1, t), INT_MAX, I32)

    def count(pred):
        def body(kt, acc):
            ks = pl.multiple_of(kt * t, t)
            hit = pred(key_ref[pl.ds(ks, t), :], rk + kt * t)
            return acc + jnp.sum(jnp.where(hit, 1, 0).reshape(t // 8, 8, t), axis=0)
        acc = lax.fori_loop(0, nk, body, jnp.zeros((8, t), I32))
        return jnp.sum(acc, axis=0, keepdims=True)

    @pl.when(c > 0)
    def _search():
        n_nonneg = count(lambda keys, idx: keys >= 0)
        thr0 = jnp.where(n_nonneg >= topk, 0, INT_MIN)

        def value_bit(j, thr):
            cand = thr + (jnp.int32(1) << (30 - j))
            n = count(lambda keys, idx: keys >= cand)
            return jnp.where(n >= topk, cand, thr)

        thr = lax.fori_loop(0, 31, value_bit, thr0)
        n_gt = count(lambda keys, idx: keys > thr)
        n_eq = count(lambda keys, idx: keys == thr)
        need = topk - n_gt
        thr_ref[0:1, :] = thr

        @pl.when(jnp.max(n_eq - need) > 0)
        def _ties():
            def index_bit(j, x):
                cand = x + (jnp.int32(1) << (11 - j))
                n = count(lambda keys, idx: (keys == thr) & (idx < cand))
                return jnp.where(n < need, cand, x)
            thr_ref[1:2, :] = lax.fori_loop(0, 12, index_bit, jnp.zeros((1, t), I32))

    thr = thr_ref[0:1, :]
    last = thr_ref[1:2, :]

    def bias_tile(kt, carry):
        ks = pl.multiple_of(kt * t, t)
        keys = key_ref[pl.ds(ks, t), :]
        tie = jnp.where(keys == thr, jnp.where(rk + kt * t <= last, 0.0, NEG_INF), NEG_INF)
        b = jnp.where(keys > thr, 0.0, tie)
        bias_ref[pl.ds(ks, t), :] = jnp.where(kt < c, b, jnp.where(vis_diag, b, NEG_INF))
        return carry

    lax.fori_loop(0, nk, bias_tile, 0)

    rel = (cq - rk).astype(F32)
    qs = [head_of_pair(q_ref, h) for h in range(A_HEADS)]
    m_ref[...] = jnp.full(m_ref.shape, -1e20, F32)
    l_ref[...] = jnp.zeros(l_ref.shape, F32)
    acc_ref[...] = jnp.zeros(acc_ref.shape, F32)

    def attend(kt, carry):
        ks = pl.multiple_of(kt * t, t)
        dist = jnp.abs(rel + ((c - kt) * t).astype(F32))
        bias = bias_ref[pl.ds(ks, t), :]

        def logits(h):
            slope = 2.0 ** (-8.0 * (h + 1) / A_HEADS) * LOG2E
            hp = slice((h // 2) * LANES, (h // 2 + 1) * LANES)
            return _dot_nt(k_ref[pl.ds(ks, t), hp], qs[h]) + (bias - slope * dist)

        _softmax_heads(logits, lambda h: vt_ref[h * A_HEAD_DIM:(h + 1) * A_HEAD_DIM, pl.ds(ks, t)],
                       A_HEADS, A_HEAD_DIM, m_ref, l_ref, acc_ref)
        return carry

    lax.fori_loop(0, nk, attend, 0)
    _softmax_finish(o_ref, l_ref, acc_ref, A_HEADS, A_HEAD_DIM)


def _dsa_attention(qki, v_t, ik2, w_t, topk):
    bsz, seq, _ = qki.shape
    t = ATT_TILE
    assert topk == t and seq % t == 0 and t == 4 * CHUNK
    return pl.pallas_call(
        functools.partial(_dsa_kernel, topk=topk),
        grid=(bsz, seq // t),
        in_specs=[pl.BlockSpec((None, t, A_WIDTH), lambda b, c: (b, c, 0)),
                  pl.BlockSpec((None, seq, A_WIDTH), lambda b, c: (b, 0, 1)),
                  pl.BlockSpec((None, t, A_WIDTH), lambda b, c: (b, c, 2)),
                  pl.BlockSpec((None, A_WIDTH, seq), lambda b, c: (b, 0, 0)),
                  pl.BlockSpec((None, seq, LANES), lambda b, c: (b, 0, 0)),
                  pl.BlockSpec((None, IDX_HEADS, t), lambda b, c: (b, 0, c))],
        out_specs=pl.BlockSpec((None, t, A_WIDTH), lambda b, c: (b, c, 0)),
        out_shape=jax.ShapeDtypeStruct((bsz, seq, A_WIDTH), BF),
        scratch_shapes=[pltpu.VMEM((seq, t), I32), pltpu.VMEM((seq, t), F32), pltpu.VMEM((8, t), I32),
                        pltpu.VMEM((A_HEADS, 8, t), F32), pltpu.VMEM((A_HEADS, 8, t), F32),
                        pltpu.VMEM((A_WIDTH, t), F32)],
        compiler_params=_params(("parallel", "arbitrary"), 40),
        name="dsa_attention",
    )(qki, qki, qki, v_t, ik2, w_t)


GDN_UNROLL = 4


GDN_PAIR = 2


def _gdn_kernel(xq_ref, xk_ref, xv_ref, cwq_ref, cwk_ref, cwv_ref, a_ref, arow_ref, b_ref, z_ref,
                alog_ref, dtb_ref, ng_ref, o_ref,
                q_s, k_s, v_s, gc_s, grow_s, u_s, wq_s, kd_s, at_s, egl_s):
    seq = xq_ref.shape[0]
    nchunk = seq // CHUNK
    cs = CHUNK
    heads = range(GDN_PAIR)
    lanes_of = lambda h: slice(h * LANES, (h + 1) * LANES)

    row = lax.broadcasted_iota(I32, (seq, LANES), 0)

    def conv_silu(x_ref, cw_ref, h):
        x = x_ref[:, lanes_of(h)]
        y = x * cw_ref[CONV_K - 1:CONV_K, lanes_of(h)]
        for j in range(1, CONV_K):
            shifted = jnp.where(row >= j, pltpu.roll(x, j, 0), 0.0)
            y = y + shifted * cw_ref[CONV_K - 1 - j:CONV_K - j, lanes_of(h)]
        return _silu(y)

    def l2n(x):
        return x * lax.rsqrt(jnp.sum(x * x, axis=-1, keepdims=True) + RMS_EPS)

    def log_decay(a, h):
        pre = a + dtb_ref[h]
        softplus = jnp.maximum(pre, 0.0) + jnp.log(1.0 + jnp.exp(-jnp.abs(pre)))
        return -jnp.exp(alog_ref[h]) * softplus

    ri = lax.broadcasted_iota(I32, (cs, cs), 0)
    ci = lax.broadcasted_iota(I32, (cs, cs), 1)
    incl = ri >= ci
    strict = ri > ci
    eye = jnp.where(ri == ci, 1.0, 0.0)
    pos = row & (cs - 1)

    for h in heads:
        q_s[h] = l2n(conv_silu(xq_ref, cwq_ref, h)) * (B_HEAD_DIM ** -0.5)
        k_s[h] = l2n(conv_silu(xk_ref, cwk_ref, h))
        v_s[h] = conv_silu(xv_ref, cwv_ref, h)
        gc = jnp.broadcast_to(log_decay(a_ref[h], h), (seq, LANES))
        for sh in (1, 2, 4, 8, 16, 32):
            gc = gc + jnp.where(pos >= sh, pltpu.roll(gc, sh, 0), 0.0)
        gc_s[h] = gc
        grow_s[h] = _dot3(log_decay(arow_ref[h], h), jnp.where(ri <= ci, 1.0, 0.0))

    def chunk_rows(n):
        return pl.ds(pl.multiple_of(n * cs, cs), cs)

    def local_group(it, carry):
        ns = [it * GDN_UNROLL + j for j in range(GDN_UNROLL)] * GDN_PAIR
        hs = [h for h in heads for _ in range(GDN_UNROLL)]
        each = lambda f, *lists: [f(*args) for args in zip(*lists)]
        rows = [chunk_rows(n) for n in ns]
        q = each(lambda h, r: q_s[h, r, :], hs, rows)
        k = each(lambda h, r: k_s[h, r, :], hs, rows)
        beta = each(lambda h, r: jax.nn.sigmoid(b_ref[h, r, :]), hs, rows)
        gcol = each(lambda h, r: gc_s[h, r, :], hs, rows)
        grow = each(lambda h, n: grow_s[h, pl.ds(n, 1), :], hs, ns)
        decay = each(lambda gc, gr: jnp.exp(jnp.where(incl, gc[:, :cs] - gr, NEG_INF)), gcol, grow)
        kb = each(lambda a, b: a * b, k, beta)
        kbf = [a.astype(BF) for a in k]
        kk = each(lambda a, b: _dot_nt(a.astype(BF), b), kb, kbf)
        x = each(lambda a, d: -jnp.where(strict, a * d, 0.0), kk, decay)
        tinv = [eye + a for a in x]
        for _ in range(5):
            x = [_dot3(a, a) for a in x]
            tinv = each(lambda tj, a: tj + _dot3(tj, a), tinv, x)
        tb = [a.astype(BF) for a in tinv]
        egc = [jnp.exp(a) for a in gcol]
        vb = each(lambda h, r, b: (v_s[h, r, :] * b).astype(BF), hs, rows, beta)
        u = each(_dot, tb, vb)
        w = each(lambda tj, a, e: _dot(tj, (a * e).astype(BF)), tb, kb, egc)
        attn = each(lambda a, b, d: jnp.where(incl, _dot_nt(a.astype(BF), b) * d, 0.0), q, kbf, decay)
        for j, (h, n) in enumerate(zip(hs, ns)):
            glast = gcol[j][cs - 1:cs, :]
            u_s[h, rows[j], :] = u[j]
            wq_s[h, pl.ds(pl.multiple_of(n * 2 * cs, 2 * cs), cs), :] = w[j].astype(BF)
            wq_s[h, pl.ds(pl.multiple_of(n * 2 * cs + cs, cs), cs), :] = (q[j] * egc[j]).astype(BF)
            at_s[h, rows[j], :] = attn[j].astype(BF)
            kd_s[h, rows[j], :] = (k[j] * jnp.exp(glast - gcol[j])).astype(BF)
            egl_s[h, pl.ds(pl.multiple_of(n * 8, 8), 8), :] = jnp.broadcast_to(jnp.exp(glast), (8, LANES))
        return carry

    lax.fori_loop(0, nchunk // GDN_UNROLL, local_group, 0)

    ng = ng_ref[...]

    def scan(n, states):
        rows = chunk_rows(n)
        sb = [s.astype(BF) for s in states]
        r = [_dot(wq_s[h, pl.ds(pl.multiple_of(n * 2 * cs, 2 * cs), 2 * cs), :], sb[h]) for h in heads]
        v_new = [(u_s[h, rows, :] - r[h][:cs, :]).astype(BF) for h in heads]
        o = [r[h][cs:, :] + _dot(at_s[h, rows, :], v_new[h]) for h in heads]
        upd = [_dot_tn(kd_s[h, rows, :], v_new[h]) for h in heads]
        new_states = []
        for h in heads:
            egl = egl_s[h, pl.ds(pl.multiple_of(n * 8, 8), 1), :]
            new_states.append(states[h] * egl + upd[h])
            on = o[h] * lax.rsqrt(jnp.mean(o[h] * o[h], axis=-1, keepdims=True) + RMS_EPS) * ng
            o_ref[rows, lanes_of(h)] = on * _silu(z_ref[rows, lanes_of(h)])
        return tuple(new_states)

    lax.fori_loop(0, nchunk, scan, tuple(jnp.zeros((B_HEAD_DIM, B_HEAD_DIM), F32) for _ in heads))


def _gdn_mixer(b_qkv, a_col, a_rows, b_col, b_z, conv_w, a_log, dt_bias, norm_g):
    bsz, seq, _ = b_qkv.shape
    nchunk = seq // CHUNK
    assert nchunk % GDN_UNROLL == 0
    p, npair = GDN_PAIR, B_HEADS // GDN_PAIR
    pw = p * LANES
    col = lambda off: pl.BlockSpec((None, seq, pw), lambda b, h: (b, 0, h + off))
    cw = lambda off: pl.BlockSpec((CONV_K, pw), lambda b, h: (0, h + off))
    gate = pl.BlockSpec((None, p, seq, 1), lambda b, h: (b, h, 0, 0))
    gate_rows = pl.BlockSpec((None, p, nchunk, CHUNK), lambda b, h: (b, h, 0, 0))
    scal = pl.BlockSpec((p, 1, 1), lambda b, h: (h, 0, 0))
    big = pltpu.VMEM((p, seq, LANES), F32)
    big_bf = pltpu.VMEM((p, seq, LANES), BF)
    return pl.pallas_call(
        _gdn_kernel,
        grid=(bsz, npair),
        in_specs=[col(0), col(npair), col(2 * npair), cw(0), cw(npair), cw(2 * npair),
                  gate, gate_rows, gate, col(0), scal, scal, _const_spec((1, B_HEAD_DIM))],
        out_specs=col(0),
        out_shape=jax.ShapeDtypeStruct((bsz, seq, B_WIDTH), F32),
        scratch_shapes=[big, big, big, big, pltpu.VMEM((p, nchunk, CHUNK), F32),
                        big, pltpu.VMEM((p, 2 * seq, LANES), BF), big_bf,
                        pltpu.VMEM((p, seq, CHUNK), BF), pltpu.VMEM((p, nchunk * 8, LANES), F32)],
        compiler_params=_params(("parallel", "parallel"), 56),
        name="gated_delta_net",
    )(b_qkv, b_qkv, b_qkv, conv_w, conv_w, conv_w, a_col, a_rows, b_col, b_z,
      a_log.reshape(B_HEADS, 1, 1), dt_bias.reshape(B_HEADS, 1, 1), norm_g.reshape(1, -1))


def _hybrid_sublayer(x, m, positions, w_in, w_out, conv_w, a_log, dt_bias, norm_g, ln_g, ln_b):
    bsz, seq, _ = x.shape
    o = EVEN_OFFS
    cols = lambda i: w_in[:, o[i]:o[i + 1]]
    w_qki = jnp.concatenate([cols(0), cols(1), cols(3)], axis=1)
    w_ik2 = jnp.concatenate([cols(4), cols(4)], axis=1)
    w_small = jnp.concatenate([cols(5), cols(7), cols(8),
                               jnp.zeros((D_MODEL, LANES - IDX_HEADS - 2 * B_HEADS), w_in.dtype)], axis=1)
    qki, a_vt, ik2, small_t, b_qkv, b_z = _in_projection(
        x, m, [w_qki, cols(2), w_ik2, w_small, cols(6), cols(9)], [BF, BF, BF, F32, F32, F32],
        transposed=(False, True, False, True, False, False),
        scale0=jnp.concatenate([jnp.full((A_WIDTH,), A_HEAD_DIM ** -0.5 * LOG2E, F32),
                                jnp.ones((2 * A_WIDTH,), F32)]))

    out_a = _dsa_attention(qki, a_vt, ik2, small_t[:, :IDX_HEADS], min(TOPK_MAX, seq // 4))

    a_t = small_t[:, IDX_HEADS:IDX_HEADS + B_HEADS]
    b_t = small_t[:, IDX_HEADS + B_HEADS:IDX_HEADS + 2 * B_HEADS]
    out_b = _gdn_mixer(b_qkv, a_t[..., None], a_t.reshape(bsz, B_HEADS, seq // CHUNK, CHUNK),
                       b_t[..., None], b_z, conv_w, a_log, dt_bias, norm_g)

    return _out_projection(x, m, [out_a, out_b], [w_out[:A_WIDTH], w_out[A_WIDTH:]], ln_g, ln_b)


def _rotate_half_cols(w):
    half = C_ROPE // 2
    return jnp.concatenate([-w[..., half:], w[..., :half]], axis=-1)


def _mla_sublayer(x, m, cos, sin, w_in, q_norm_g, w_q_up, kv_norm_g, w_kv_up, w_out, ln_g, ln_b):
    pad = LANES - C_NOPE - C_ROPE
    w_kr = w_in[:, C_Q_RANK + C_KV_RANK:]
    place = lambda w: jnp.pad(w, ((0, 0), (C_NOPE, pad)))
    c_q, c_kv, k_r, k_rot = _in_projection(
        x, m, [w_in[:, :C_Q_RANK], w_in[:, C_Q_RANK:C_Q_RANK + C_KV_RANK], place(w_kr),
               place(_rotate_half_cols(w_kr))], [F32, F32, F32, F32])

    wq = w_q_up.reshape(C_Q_RANK, C_HEADS, C_NOPE + C_ROPE)
    wq_main = jnp.pad(wq, ((0, 0), (0, 0), (0, pad))).reshape(C_Q_RANK, -1)
    wq_rot = jnp.pad(_rotate_half_cols(wq[..., C_NOPE:]), ((0, 0), (0, 0), (C_NOPE, pad))).reshape(C_Q_RANK, -1)
    wkv = w_kv_up.reshape(C_KV_RANK, C_HEADS, C_NOPE + C_V)
    wk = jnp.pad(wkv[..., :C_NOPE], ((0, 0), (0, 0), (0, LANES - C_NOPE))).reshape(C_KV_RANK, -1)
    wv = wkv[..., C_NOPE:].reshape(C_KV_RANK, -1)

    q, k, v = _mla_qkv(c_q, c_kv, k_r, k_rot, cos, sin, q_norm_g, kv_norm_g, wq_main, wq_rot, wk, wv)
    out = _mla_attention(q, k, v)
    return _out_projection(x, m, [out], [w_out], ln_g, ln_b)


def kernel(x, c, positions, mod_w, mod_b, ln_g, ln_b, ffn_w_gate, ffn_w_up, ffn_w_down, hyb_w_in, hyb_w_out, gdn_conv_w, gdn_a_log, gdn_dt_bias, gdn_norm_g, mla_w_in, mla_q_norm_g, mla_w_q_up, mla_kv_norm_g, mla_w_kv_up, mla_w_out):
    bsz = x.shape[0]
    mod = _modulation(c, mod_w, mod_b).reshape(DEPTH, bsz, N_SUB, 3, D_MODEL)
    cos, sin = _rope_tables(positions)
    for layer in range(DEPTH):
        x = _ffn_sublayer(x, mod[layer, :, 0], ffn_w_gate[layer, 0], ffn_w_up[layer, 0], ffn_w_down[layer, 0],
                          ln_g[layer, 0], ln_b[layer, 0])
        if layer % 2 == 0:
            e = layer // 2
            x = _hybrid_sublayer(x, mod[layer, :, 1], positions, hyb_w_in[e], hyb_w_out[e], gdn_conv_w[e],
                                 gdn_a_log[e], gdn_dt_bias[e], gdn_norm_g[e], ln_g[layer, 1], ln_b[layer, 1])
        else:
            o = layer // 2
            x = _mla_sublayer(x, mod[layer, :, 1], cos, sin, mla_w_in[o], mla_q_norm_g[o], mla_w_q_up[o],
                              mla_kv_norm_g[o], mla_w_kv_up[o], mla_w_out[o], ln_g[layer, 1], ln_b[layer, 1])
        x = _ffn_sublayer(x, mod[layer, :, 2], ffn_w_gate[layer, 1], ffn_w_up[layer, 1], ffn_w_down[layer, 1],
                          ln_g[layer, 2], ln_b[layer, 2])
    return x
```

```python
import functools

import numpy as np
import jax
import jax.numpy as jnp
from jax import lax
from jax.experimental import pallas as pl
from jax.experimental.pallas import tpu as pltpu

BF = jnp.bfloat16
F32 = jnp.float32
I32 = jnp.int32

D_MODEL = 1024
DEPTH = 4
CHUNK = 64
DEEPNORM_ALPHA = (2.0 * DEPTH) ** 0.25
D_FF = 2816
N_SUB = 3
LN_EPS = 1e-5
RMS_EPS = 1e-6
NEG_INF = -1e30

A_HEADS = 8
A_HEAD_DIM = 64
A_WIDTH = A_HEADS * A_HEAD_DIM
IDX_HEADS = 8
IDX_DIM = 64
TOPK_MAX = 256

B_HEADS = 4
B_HEAD_DIM = 128
B_WIDTH = B_HEADS * B_HEAD_DIM
CONV_K = 4

EVEN_SIZES = (A_WIDTH, A_WIDTH, A_WIDTH, IDX_HEADS * IDX_DIM, IDX_DIM, IDX_HEADS,
              3 * B_WIDTH, B_HEADS, B_HEADS, B_WIDTH)
EVEN_OFFS = tuple(int(s) for s in np.cumsum((0,) + EVEN_SIZES))

C_HEADS = 16
C_Q_RANK = 384
C_KV_RANK = 256
C_NOPE = 64
C_ROPE = 32
C_V = 64
ROPE_THETA = 10000.0

LANES = 128
ROW_TILE = 512
MXU_WIDTH = 256
FF_SPLITS = (0, 6 * MXU_WIDTH, D_FF)
ATT_TILE = 256
LOG2E = 1.4426950408889634
INT_MIN = -(2 ** 31)
INT_MAX = 2 ** 31 - 1


def _params(sem, vmem_mb):
    return pltpu.CompilerParams(dimension_semantics=sem, vmem_limit_bytes=vmem_mb << 20)


def _const_spec(shape):
    nd = len(shape)
    return pl.BlockSpec(shape, lambda *_: (0,) * nd, pipeline_mode=pl.Buffered(1))


def _dot(a, b):
    return jnp.dot(a, b, preferred_element_type=F32)


def _dot_nt(a, b):
    return lax.dot_general(a, b, (((1,), (1,)), ((), ())), preferred_element_type=F32)


def _dot_tn(a, b):
    return lax.dot_general(a, b, (((0,), (0,)), ((), ())), preferred_element_type=F32)


def _dot3(a, b):
    a_hi = a.astype(BF)
    b_hi = b.astype(BF)
    a_lo = (a - a_hi.astype(F32)).astype(BF)
    b_lo = (b - b_hi.astype(F32)).astype(BF)
    return _dot(a_hi, b_hi) + (_dot(a_hi, b_lo) + _dot(a_lo, b_hi))


def _silu(x):
    return x * jax.nn.sigmoid(x)


def _layer_norm(z, g, b):
    mu = jnp.mean(z, axis=-1, keepdims=True)
    zc = z - mu
    var = jnp.mean(zc * zc, axis=-1, keepdims=True)
    return zc * lax.rsqrt(var + LN_EPS) * g + b


def _modulate(x, m_ref):
    return x * (1.0 + m_ref[1:2, :]) + m_ref[0:1, :]


def _mod_kernel(c_ref, w_ref, b_ref, o_ref):
    cs = _silu(c_ref[...]).astype(BF)
    o_ref[...] = _dot(cs, w_ref[...].astype(BF)) + b_ref[...]


def _modulation(c, mod_w, mod_b):
    bsz = c.shape[0]
    n = mod_w.shape[-1]
    tn = 1536
    return pl.pallas_call(
        _mod_kernel,
        grid=(DEPTH, n // tn),
        in_specs=[pl.BlockSpec((bsz, D_MODEL), lambda l, j: (0, 0)),
                  pl.BlockSpec((None, D_MODEL, tn), lambda l, j: (l, 0, j)),
                  pl.BlockSpec((None, 1, tn), lambda l, j: (l, 0, j))],
        out_specs=pl.BlockSpec((None, bsz, tn), lambda l, j: (l, 0, j)),
        out_shape=jax.ShapeDtypeStruct((DEPTH, bsz, n), F32),
        compiler_params=_params(("arbitrary", "arbitrary"), 40),
        name="adaln_modulation",
    )(c, mod_w, mod_b.reshape(DEPTH, 1, n))


def _ffn_kernel(*refs, n_mix):
    if n_mix:
        x_ref, mo_ref = refs[:2]
        mix_refs, wo_refs = refs[2:2 + n_mix], refs[2 + n_mix:2 + 2 * n_mix]
        go_ref, bo_ref, m_ref, wg_ref, wu_ref, wd_ref, g_ref, b_ref, o_ref, acc_ref, xin_ref = refs[2 + 2 * n_mix:]
    else:
        xin_ref, m_ref, wg_ref, wu_ref, wd_ref, g_ref, b_ref, o_ref, acc_ref = refs
    tm = o_ref.shape[0]
    halves = [slice(0, tm // 2), slice(tm // 2, tm)]
    if n_mix:
        ys = [sum(_dot(mx[r, :].astype(BF), w[...]) for mx, w in zip(mix_refs, wo_refs)) for r in halves]
        for r, y in zip(halves, ys):
            z = DEEPNORM_ALPHA * x_ref[r, :] + (1.0 + mo_ref[2:3, :]) * y
            xin_ref[r, :] = _layer_norm(z, go_ref[...], bo_ref[...])
    x_ref = xin_ref
    us = [_modulate(x_ref[r, :], m_ref).astype(BF) for r in halves]
    for c, (lo, hi) in enumerate(zip(FF_SPLITS[:-1], FF_SPLITS[1:])):
        sl = slice(lo, hi)
        hg = [_dot(u, wg_ref[:, sl]) for u in us]
        hu = [_dot(u, wu_ref[:, sl]) for u in us]
        hs = [(_silu(a) * b).astype(BF) for a, b in zip(hg, hu)]
        for r, h in zip(halves, hs):
            part = _dot(h, wd_ref[sl, :])
            if c == 0:
                acc_ref[r, :] = part
            else:
                acc_ref[r, :] += part
    for r in halves:
        z = DEEPNORM_ALPHA * x_ref[r, :] + (1.0 + m_ref[2:3, :]) * (0.5 * acc_ref[r, :])
        o_ref[r, :] = _layer_norm(z, g_ref[...], b_ref[...])


def _ffn_sublayer(x, m, wg, wu, wd, ln_g, ln_b, mixer=None):
    bsz, seq, _ = x.shape
    tm = min(ROW_TILE, seq)
    row = lambda width: pl.BlockSpec((None, tm, width), lambda b, i: (b, i, 0))
    mod = pl.BlockSpec((None, 3, D_MODEL), lambda b, i: (b, 0, 0))
    vec = _const_spec((1, D_MODEL))
    ffn_specs = [mod, _const_spec((D_MODEL, D_FF)), _const_spec((D_MODEL, D_FF)), _const_spec((D_FF, D_MODEL)),
                 vec, vec]
    ffn_args = (m, wg.astype(BF), wu.astype(BF), wd.astype(BF), ln_g.reshape(1, -1), ln_b.reshape(1, -1))
    scratch = [pltpu.VMEM((tm, D_MODEL), F32)]
    if mixer is None:
        n_mix, pre_specs, pre_args = 0, [row(D_MODEL)], (x,)
    else:
        m_o, mixes, w_outs, g_o, b_o = mixer
        n_mix = len(mixes)
        pre_specs = ([row(D_MODEL), mod] + [row(mx.shape[-1]) for mx in mixes]
                     + [_const_spec(w.shape) for w in w_outs] + [vec, vec])
        pre_args = (x, m_o, *mixes, *[w.astype(BF) for w in w_outs], g_o.reshape(1, -1), b_o.reshape(1, -1))
        scratch.append(pltpu.VMEM((tm, D_MODEL), F32))
    return pl.pallas_call(
        functools.partial(_ffn_kernel, n_mix=n_mix),
        grid=(bsz, seq // tm),
        in_specs=pre_specs + ffn_specs,
        out_specs=row(D_MODEL),
        out_shape=jax.ShapeDtypeStruct(x.shape, F32),
        scratch_shapes=scratch,
        compiler_params=_params(("parallel", "parallel"), 56),
        name="ffn_sublayer",
    )(*pre_args, *ffn_args)


def _inproj_kernel(x_ref, m_ref, s_ref, *refs, n, transposed):
    u = _modulate(x_ref[...], m_ref).astype(BF)
    for g, (w_ref, o_ref, tr) in enumerate(zip(refs[:n], refs[n:], transposed)):
        y = _dot_nt(w_ref[...], u) if tr else _dot(u, w_ref[...])
        if g == 0:
            y = y * s_ref[...]
        o_ref[...] = y.astype(o_ref.dtype)


def _in_projection(x, m, weights, dtypes, transposed=None, scale0=None):
    bsz, seq, _ = x.shape
    tm = min(ROW_TILE, seq)
    n = len(weights)
    transposed = transposed or (False,) * n
    assert not transposed[0]
    n0 = weights[0].shape[1]
    scale0 = jnp.ones((n0,), F32) if scale0 is None else scale0
    row = lambda width: pl.BlockSpec((None, tm, width), lambda b, i: (b, i, 0))
    colb = lambda width: pl.BlockSpec((None, width, tm), lambda b, i: (b, 0, i))
    ws = [(w.T if tr else w).astype(BF) for w, tr in zip(weights, transposed)]
    return pl.pallas_call(
        functools.partial(_inproj_kernel, n=n, transposed=tuple(transposed)),
        grid=(bsz, seq // tm),
        in_specs=[row(D_MODEL), pl.BlockSpec((None, 3, D_MODEL), lambda b, i: (b, 0, 0)),
                  _const_spec((1, n0))] + [_const_spec(w.shape) for w in ws],
        out_specs=[colb(w.shape[1]) if tr else row(w.shape[1]) for w, tr in zip(weights, transposed)],
        out_shape=[jax.ShapeDtypeStruct((bsz, w.shape[1], seq) if tr else (bsz, seq, w.shape[1]), dt)
                   for w, dt, tr in zip(weights, dtypes, transposed)],
        compiler_params=_params(("parallel", "parallel"), 48),
        name="in_projection",
    )(x, m, scale0.reshape(1, n0), *ws)


def _rope_table_kernel(pos_ref, invf_ref, cos_ref, sin_ref):
    ang = pos_ref[...].astype(F32) * invf_ref[...]
    lane = lax.broadcasted_iota(I32, ang.shape, 1)
    rot = (lane >= C_NOPE) & (lane < C_NOPE + C_ROPE)
    cos_ref[...] = jnp.where(lane < C_NOPE, 1.0, jnp.where(rot, jnp.cos(ang), 0.0))
    sin_ref[...] = jnp.where(rot, jnp.sin(ang), 0.0)


def _rope_tables(positions):
    bsz, seq = positions.shape
    tm = min(ROW_TILE, seq)
    half = C_ROPE // 2
    inv_freq = ROPE_THETA ** (-jnp.arange(half, dtype=F32) / half)
    invf = jnp.zeros((1, LANES), F32).at[0, C_NOPE:C_NOPE + C_ROPE].set(jnp.tile(inv_freq, 2))
    tab = jax.ShapeDtypeStruct((bsz, seq, LANES), F32)
    return pl.pallas_call(
        _rope_table_kernel,
        grid=(bsz, seq // tm),
        in_specs=[pl.BlockSpec((None, tm, 1), lambda b, i: (b, i, 0)), _const_spec((1, LANES))],
        out_specs=[pl.BlockSpec((None, tm, LANES), lambda b, i: (b, i, 0))] * 2,
        out_shape=[tab, tab],
        compiler_params=_params(("parallel", "parallel"), 32),
        name="rope_tables",
    )(positions.reshape(bsz, seq, 1), invf)


def _rms_norm(x, g):
    return x * lax.rsqrt(jnp.mean(x * x, axis=-1, keepdims=True) + RMS_EPS) * g


def _mla_qkv_kernel(cq_ref, ckv_ref, kr_ref, krot_ref, cos_ref, sin_ref, qg_ref, kvg_ref,
                    wq_ref, wqrot_ref, wk_ref, wvt_ref, q_ref, k_ref, vt_ref, *, scale):
    cos = cos_ref[...]
    sin = sin_ref[...]
    cq = _rms_norm(cq_ref[...], qg_ref[...]).astype(BF)
    ckv = _rms_norm(ckv_ref[...], kvg_ref[...]).astype(BF)
    k_rope = kr_ref[...] * cos + krot_ref[...] * sin
    qcos = cos * scale
    qsin = sin * scale
    per = MXU_WIDTH // LANES
    for g in range(C_HEADS // per):
        wide = slice(g * MXU_WIDTH, (g + 1) * MXU_WIDTH)
        q_main, q_rot, k_nope = _dot(cq, wq_ref[:, wide]), _dot(cq, wqrot_ref[:, wide]), _dot(ckv, wk_ref[:, wide])
        for r in range(per):
            part = slice(r * LANES, (r + 1) * LANES)
            sl = slice(g * MXU_WIDTH + r * LANES, g * MXU_WIDTH + (r + 1) * LANES)
            q_ref[:, sl] = (q_main[:, part] * qcos + q_rot[:, part] * qsin).astype(BF)
            k_ref[:, sl] = (k_nope[:, part] + k_rope).astype(BF)
    vt_ref[...] = _dot_nt(wvt_ref[...], ckv).astype(BF)


def _mla_qkv(c_q, c_kv, k_r, k_rot, cos, sin, qg, kvg, wq, wqrot, wk, wv):
    bsz, seq, _ = c_q.shape
    tm = min(ROW_TILE, seq)
    row = lambda width: pl.BlockSpec((None, tm, width), lambda b, i: (b, i, 0))
    hw = C_HEADS * LANES
    vw = C_HEADS * C_V
    return pl.pallas_call(
        functools.partial(_mla_qkv_kernel, scale=(C_NOPE + C_ROPE) ** -0.5 * LOG2E),
        grid=(bsz, seq // tm),
        in_specs=[row(C_Q_RANK), row(C_KV_RANK), row(LANES), row(LANES), row(LANES), row(LANES),
                  _const_spec((1, C_Q_RANK)), _const_spec((1, C_KV_RANK)),
                  _const_spec(wq.shape), _const_spec(wqrot.shape), _const_spec(wk.shape),
                  _const_spec((vw, C_KV_RANK))],
        out_specs=[row(hw), row(hw), pl.BlockSpec((None, vw, tm), lambda b, i: (b, 0, i))],
        out_shape=[jax.ShapeDtypeStruct((bsz, seq, hw), BF),
                   jax.ShapeDtypeStruct((bsz, seq, hw), BF),
                   jax.ShapeDtypeStruct((bsz, vw, seq), BF)],
        compiler_params=_params(("parallel", "parallel"), 40),
        name="mla_qkv",
    )(c_q, c_kv, k_r, k_rot, cos, sin, qg.reshape(1, -1), kvg.reshape(1, -1),
      wq.astype(BF), wqrot.astype(BF), wk.astype(BF), wv.T.astype(BF))


def _softmax_heads(logits_fn, values_fn, nheads, dv, m_ref, l_ref, acc_ref):
    ahead = 3
    pending = [logits_fn(h) for h in range(min(ahead, nheads))]
    for h in range(nheads):
        s = pending.pop(0)
        if h + ahead < nheads:
            pending.append(logits_fn(h + ahead))
        rows = slice(h * dv, (h + 1) * dv)
        m_old = m_ref[h]
        m_new = jnp.maximum(m_old, jnp.max(s, axis=0, keepdims=True))
        alpha = jnp.exp2(m_old - m_new)
        p = jnp.exp2(s - m_new[0:1, :])
        l_ref[h] = alpha * l_ref[h] + jnp.sum(p, axis=0, keepdims=True)
        m_ref[h] = m_new
        acc_ref[rows, :] = alpha[0:1, :] * acc_ref[rows, :] + _dot(values_fn(h), p.astype(BF))


def _softmax_finish(o_ref, l_ref, acc_ref, nheads, dv):
    per = LANES // dv
    for j in range(nheads // per):
        parts = [acc_ref[(per * j + r) * dv:(per * j + r + 1) * dv, :] / l_ref[per * j + r][0:1, :]
                 for r in range(per)]
        o_ref[:, j * LANES:(j + 1) * LANES] = jnp.concatenate(parts, axis=0).T.astype(o_ref.dtype)


MLA_GROUP = 8


def _mla_attn_kernel(q_ref, k_ref, vt_ref, o_ref, m_ref, l_ref, acc_ref):
    t = ATT_TILE
    i = pl.program_id(2)
    rk = lax.broadcasted_iota(I32, (t, t), 0)
    cq = lax.broadcasted_iota(I32, (t, t), 1)
    diag_bias = jnp.where((rk >> 6) <= (cq >> 6), 0.0, NEG_INF)
    m_ref[...] = jnp.full(m_ref.shape, NEG_INF, F32)
    l_ref[...] = jnp.zeros(l_ref.shape, F32)
    acc_ref[...] = jnp.zeros(acc_ref.shape, F32)

    def step(kt, bias):
        ks = pl.multiple_of(kt * t, t)

        def logits(h):
            hs = slice(h * LANES, (h + 1) * LANES)
            s = _dot_nt(k_ref[pl.ds(ks, t), hs], q_ref[:, hs])
            return s if bias is None else s + bias

        _softmax_heads(logits, lambda h: vt_ref[h * C_V:(h + 1) * C_V, pl.ds(ks, t)],
                       MLA_GROUP, C_V, m_ref, l_ref, acc_ref)

    def body(kt, carry):
        step(kt, None)
        return carry

    lax.fori_loop(0, i, body, 0)
    step(i, diag_bias)
    _softmax_finish(o_ref, l_ref, acc_ref, MLA_GROUP, C_V)


def _mla_attention(q, k, v_t):
    bsz, seq, _ = q.shape
    t = ATT_TILE
    g = MLA_GROUP
    return pl.pallas_call(
        _mla_attn_kernel,
        grid=(bsz, C_HEADS // g, seq // t),
        in_specs=[pl.BlockSpec((None, t, g * LANES), lambda b, h, i: (b, i, h)),
                  pl.BlockSpec((None, seq, g * LANES), lambda b, h, i: (b, 0, h)),
                  pl.BlockSpec((None, g * C_V, seq), lambda b, h, i: (b, h, 0))],
        out_specs=pl.BlockSpec((None, t, g * C_V), lambda b, h, i: (b, i, h)),
        out_shape=jax.ShapeDtypeStruct((bsz, seq, C_HEADS * C_V), BF),
        scratch_shapes=[pltpu.VMEM((g, 8, t), F32), pltpu.VMEM((g, 8, t), F32), pltpu.VMEM((g * C_V, t), F32)],
        compiler_params=_params(("parallel", "parallel", "arbitrary"), 40),
        name="mla_attention",
    )(q, k, v_t)


def _order_key(s):
    b = lax.bitcast_convert_type(s, I32)
    return b ^ ((b >> 31) & INT_MAX)


def _dsa_kernel(q_ref, k_ref, iq_ref, vt_ref, ik_ref, wt_ref, o_ref, key_ref, bias_ref, thr_ref,
                m_ref, l_ref, acc_ref, *, topk):
    t = ATT_TILE
    c = pl.program_id(1)
    nk = c + 1
    rk = lax.broadcasted_iota(I32, (t, t), 0)
    cq = lax.broadcasted_iota(I32, (t, t), 1)
    vis_diag = (rk >> 6) <= (cq >> 6)
    low_half = lax.broadcasted_iota(I32, (t, LANES), 1) < A_HEAD_DIM

    def head_of_pair(ref, h):
        blk = ref[:, (h // 2) * LANES:(h // 2 + 1) * LANES]
        keep = low_half if h % 2 == 0 else jnp.logical_not(low_half)
        return jnp.where(keep, blk, jnp.zeros_like(blk))

    coef = wt_ref[...] * (IDX_HEADS ** -0.5 * IDX_DIM ** -0.5)
    iq = [head_of_pair(iq_ref, h) for h in range(IDX_HEADS)]

    def score_tile(kt, carry):
        ks = pl.multiple_of(kt * t, t)
        ik = ik_ref[pl.ds(ks, t), :]
        s = jnp.zeros((t, t), F32)
        for h in range(IDX_HEADS):
            s = s + jnp.maximum(_dot_nt(ik, iq[h]), 0.0) * coef[h:h + 1, :]
        s = jnp.where(kt < c, s, jnp.where(vis_diag, s, NEG_INF))
        key_ref[pl.ds(ks, t), :] = _order_key(s)
        return carry

    lax.fori_loop(0, nk, score_tile, 0)

    thr_ref[0:1, :] = jnp.full((1, t), INT_MIN, I32)
    thr_ref[1:2, :] = jnp.full((1, t), INT_MAX, I32)

    def count(pred):
        def body(kt, acc):
            ks = pl.multiple_of(kt * t, t)
            hit = pred(key_ref[pl.ds(ks, t), :], rk + kt * t)
            return acc + jnp.sum(jnp.where(hit, 1, 0).reshape(t // 8, 8, t), axis=0)
        acc = lax.fori_loop(0, nk, body, jnp.zeros((8, t), I32))
        return jnp.sum(acc, axis=0, keepdims=True)

    @pl.when(c > 0)
    def _search():
        n_nonneg = count(lambda keys, idx: keys >= 0)
        thr0 = jnp.where(n_nonneg >= topk, 0, INT_MIN)

        def value_bit(j, thr):
            cand = thr + (jnp.int32(1) << (30 - j))
            n = count(lambda keys, idx: keys >= cand)
            return jnp.where(n >= topk, cand, thr)

        thr = lax.fori_loop(0, 31, value_bit, thr0)
        n_gt = count(lambda keys, idx: keys > thr)
        n_eq = count(lambda keys, idx: keys == thr)
        need = topk - n_gt
        thr_ref[0:1, :] = thr

        @pl.when(jnp.max(n_eq - need) > 0)
        def _ties():
            def index_bit(j, x):
                cand = x + (jnp.int32(1) << (11 - j))
                n = count(lambda keys, idx: (keys == thr) & (idx < cand))
                return jnp.where(n < need, cand, x)
            thr_ref[1:2, :] = lax.fori_loop(0, 12, index_bit, jnp.zeros((1, t), I32))

    thr = thr_ref[0:1, :]
    last = thr_ref[1:2, :]

    def bias_tile(kt, carry):
        ks = pl.multiple_of(kt * t, t)
        keys = key_ref[pl.ds(ks, t), :]
        tie = jnp.where(keys == thr, jnp.where(rk + kt * t <= last, 0.0, NEG_INF), NEG_INF)
        b = jnp.where(keys > thr, 0.0, tie)
        bias_ref[pl.ds(ks, t), :] = jnp.where(kt < c, b, jnp.where(vis_diag, b, NEG_INF))
        return carry

    lax.fori_loop(0, nk, bias_tile, 0)

    rel = (cq - rk).astype(F32)
    qs = [head_of_pair(q_ref, h) for h in range(A_HEADS)]
    m_ref[...] = jnp.full(m_ref.shape, -1e20, F32)
    l_ref[...] = jnp.zeros(l_ref.shape, F32)
    acc_ref[...] = jnp.zeros(acc_ref.shape, F32)

    def attend(kt, carry):
        ks = pl.multiple_of(kt * t, t)
        dist = jnp.abs(rel + ((c - kt) * t).astype(F32))
        bias = bias_ref[pl.ds(ks, t), :]

        def logits(h):
            slope = 2.0 ** (-8.0 * (h + 1) / A_HEADS) * LOG2E
            hp = slice((h // 2) * LANES, (h // 2 + 1) * LANES)
            return _dot_nt(k_ref[pl.ds(ks, t), hp], qs[h]) + (bias - slope * dist)

        _softmax_heads(logits, lambda h: vt_ref[h * A_HEAD_DIM:(h + 1) * A_HEAD_DIM, pl.ds(ks, t)],
                       A_HEADS, A_HEAD_DIM, m_ref, l_ref, acc_ref)
        return carry

    lax.fori_loop(0, nk, attend, 0)
    _softmax_finish(o_ref, l_ref, acc_ref, A_HEADS, A_HEAD_DIM)


def _dsa_attention(qki, v_t, ik2, w_t, topk):
    bsz, seq, _ = qki.shape
    t = ATT_TILE
    assert topk == t and seq % t == 0 and t == 4 * CHUNK
    return pl.pallas_call(
        functools.partial(_dsa_kernel, topk=topk),
        grid=(bsz, seq // t),
        in_specs=[pl.BlockSpec((None, t, A_WIDTH), lambda b, c: (b, c, 0)),
                  pl.BlockSpec((None, seq, A_WIDTH), lambda b, c: (b, 0, 1)),
                  pl.BlockSpec((None, t, A_WIDTH), lambda b, c: (b, c, 2)),
                  pl.BlockSpec((None, A_WIDTH, seq), lambda b, c: (b, 0, 0)),
                  pl.BlockSpec((None, seq, LANES), lambda b, c: (b, 0, 0)),
                  pl.BlockSpec((None, IDX_HEADS, t), lambda b, c: (b, 0, c))],
        out_specs=pl.BlockSpec((None, t, A_WIDTH), lambda b, c: (b, c, 0)),
        out_shape=jax.ShapeDtypeStruct((bsz, seq, A_WIDTH), BF),
        scratch_shapes=[pltpu.VMEM((seq, t), I32), pltpu.VMEM((seq, t), F32), pltpu.VMEM((8, t), I32),
                        pltpu.VMEM((A_HEADS, 8, t), F32), pltpu.VMEM((A_HEADS, 8, t), F32),
                        pltpu.VMEM((A_WIDTH, t), F32)],
        compiler_params=_params(("parallel", "arbitrary"), 40),
        name="dsa_attention",
    )(qki, qki, qki, v_t, ik2, w_t)


GDN_UNROLL = 4


GDN_PAIR = 2


def _gdn_kernel(xq_ref, xk_ref, xv_ref, cwq_ref, cwk_ref, cwv_ref, a_ref, arow_ref, b_ref, z_ref,
                alog_ref, dtb_ref, ng_ref, o_ref,
                q_s, k_s, v_s, gc_s, grow_s, u_s, wq_s, kd_s, at_s, egl_s):
    seq = xq_ref.shape[0]
    nchunk = seq // CHUNK
    cs = CHUNK
    heads = range(GDN_PAIR)
    lanes_of = lambda h: slice(h * LANES, (h + 1) * LANES)

    row = lax.broadcasted_iota(I32, (seq, LANES), 0)

    def conv_silu(x_ref, cw_ref, h):
        x = x_ref[:, lanes_of(h)]
        y = x * cw_ref[CONV_K - 1:CONV_K, lanes_of(h)]
        for j in range(1, CONV_K):
            shifted = jnp.where(row >= j, pltpu.roll(x, j, 0), 0.0)
            y = y + shifted * cw_ref[CONV_K - 1 - j:CONV_K - j, lanes_of(h)]
        return _silu(y)

    def l2n(x):
        return x * lax.rsqrt(jnp.sum(x * x, axis=-1, keepdims=True) + RMS_EPS)

    def log_decay(a, h):
        pre = a + dtb_ref[h]
        softplus = jnp.maximum(pre, 0.0) + jnp.log(1.0 + jnp.exp(-jnp.abs(pre)))
        return -jnp.exp(alog_ref[h]) * softplus

    ri = lax.broadcasted_iota(I32, (cs, cs), 0)
    ci = lax.broadcasted_iota(I32, (cs, cs), 1)
    incl = ri >= ci
    strict = ri > ci
    eye = jnp.where(ri == ci, 1.0, 0.0)
    pos = row & (cs - 1)

    for h in heads:
        q_s[h] = l2n(conv_silu(xq_ref, cwq_ref, h)) * (B_HEAD_DIM ** -0.5)
        k_s[h] = l2n(conv_silu(xk_ref, cwk_ref, h))
        v_s[h] = conv_silu(xv_ref, cwv_ref, h)
        gc = jnp.broadcast_to(log_decay(a_ref[h], h), (seq, LANES))
        for sh in (1, 2, 4, 8, 16, 32):
            gc = gc + jnp.where(pos >= sh, pltpu.roll(gc, sh, 0), 0.0)
        gc_s[h] = gc
        grow_s[h] = _dot3(log_decay(arow_ref[h], h), jnp.where(ri <= ci, 1.0, 0.0))

    def chunk_rows(n):
        return pl.ds(pl.multiple_of(n * cs, cs), cs)

    def local_group(it, carry):
        ns = [it * GDN_UNROLL + j for j in range(GDN_UNROLL)] * GDN_PAIR
        hs = [h for h in heads for _ in range(GDN_UNROLL)]
        each = lambda f, *lists: [f(*args) for args in zip(*lists)]
        rows = [chunk_rows(n) for n in ns]
        q = each(lambda h, r: q_s[h, r, :], hs, rows)
        k = each(lambda h, r: k_s[h, r, :], hs, rows)
        beta = each(lambda h, r: jax.nn.sigmoid(b_ref[h, r, :]), hs, rows)
        gcol = each(lambda h, r: gc_s[h, r, :], hs, rows)
        grow = each(lambda h, n: grow_s[h, pl.ds(n, 1), :], hs, ns)
        decay = each(lambda gc, gr: jnp.exp(jnp.where(incl, gc[:, :cs] - gr, NEG_INF)), gcol, grow)
        kb = each(lambda a, b: a * b, k, beta)
        kbf = [a.astype(BF) for a in k]
        kk = each(lambda a, b: _dot_nt(a.astype(BF), b), kb, kbf)
        x = each(lambda a, d: -jnp.where(strict, a * d, 0.0), kk, decay)
        tinv = [eye + a for a in x]
        for _ in range(5):
            x = [_dot3(a, a) for a in x]
            tinv = each(lambda tj, a: tj + _dot3(tj, a), tinv, x)
        tb = [a.astype(BF) for a in tinv]
        egc = [jnp.exp(a) for a in gcol]
        vb = each(lambda h, r, b: (v_s[h, r, :] * b).astype(BF), hs, rows, beta)
        u = each(_dot, tb, vb)
        w = each(lambda tj, a, e: _dot(tj, (a * e).astype(BF)), tb, kb, egc)
        attn = each(lambda a, b, d: jnp.where(incl, _dot_nt(a.astype(BF), b) * d, 0.0), q, kbf, decay)
        for j, (h, n) in enumerate(zip(hs, ns)):
            glast = gcol[j][cs - 1:cs, :]
            u_s[h, rows[j], :] = u[j]
            wq_s[h, pl.ds(pl.multiple_of(n * 2 * cs, 2 * cs), cs), :] = w[j].astype(BF)
            wq_s[h, pl.ds(pl.multiple_of(n * 2 * cs + cs, cs), cs), :] = (q[j] * egc[j]).astype(BF)
            at_s[h, rows[j], :] = attn[j].astype(BF)
            kd_s[h, rows[j], :] = (k[j] * jnp.exp(glast - gcol[j])).astype(BF)
            egl_s[h, pl.ds(pl.multiple_of(n * 8, 8), 8), :] = jnp.broadcast_to(jnp.exp(glast), (8, LANES))
        return carry

    lax.fori_loop(0, nchunk // GDN_UNROLL, local_group, 0)

    ng = ng_ref[...]

    def scan(n, states):
        rows = chunk_rows(n)
        sb = [s.astype(BF) for s in states]
        r = [_dot(wq_s[h, pl.ds(pl.multiple_of(n * 2 * cs, 2 * cs), 2 * cs), :], sb[h]) for h in heads]
        v_new = [(u_s[h, rows, :] - r[h][:cs, :]).astype(BF) for h in heads]
        o = [r[h][cs:, :] + _dot(at_s[h, rows, :], v_new[h]) for h in heads]
        upd = [_dot_tn(kd_s[h, rows, :], v_new[h]) for h in heads]
        new_states = []
        for h in heads:
            egl = egl_s[h, pl.ds(pl.multiple_of(n * 8, 8), 1), :]
            new_states.append(states[h] * egl + upd[h])
            on = o[h] * lax.rsqrt(jnp.mean(o[h] * o[h], axis=-1, keepdims=True) + RMS_EPS) * ng
            o_ref[rows, lanes_of(h)] = on * _silu(z_ref[rows, lanes_of(h)])
        return tuple(new_states)

    lax.fori_loop(0, nchunk, scan, tuple(jnp.zeros((B_HEAD_DIM, B_HEAD_DIM), F32) for _ in heads))


def _gdn_mixer(b_qkv, a_col, a_rows, b_col, b_z, conv_w, a_log, dt_bias, norm_g):
    bsz, seq, _ = b_qkv.shape
    nchunk = seq // CHUNK
    assert nchunk % GDN_UNROLL == 0
    p, npair = GDN_PAIR, B_HEADS // GDN_PAIR
    pw = p * LANES
    col = lambda off: pl.BlockSpec((None, seq, pw), lambda b, h: (b, 0, h + off))
    cw = lambda off: pl.BlockSpec((CONV_K, pw), lambda b, h: (0, h + off))
    gate = pl.BlockSpec((None, p, seq, 1), lambda b, h: (b, h, 0, 0))
    gate_rows = pl.BlockSpec((None, p, nchunk, CHUNK), lambda b, h: (b, h, 0, 0))
    scal = pl.BlockSpec((p, 1, 1), lambda b, h: (h, 0, 0))
    big = pltpu.VMEM((p, seq, LANES), F32)
    big_bf = pltpu.VMEM((p, seq, LANES), BF)
    return pl.pallas_call(
        _gdn_kernel,
        grid=(bsz, npair),
        in_specs=[col(0), col(npair), col(2 * npair), cw(0), cw(npair), cw(2 * npair),
                  gate, gate_rows, gate, col(0), scal, scal, _const_spec((1, B_HEAD_DIM))],
        out_specs=col(0),
        out_shape=jax.ShapeDtypeStruct((bsz, seq, B_WIDTH), F32),
        scratch_shapes=[big, big, big, big, pltpu.VMEM((p, nchunk, CHUNK), F32),
                        big, pltpu.VMEM((p, 2 * seq, LANES), BF), big_bf,
                        pltpu.VMEM((p, seq, CHUNK), BF), pltpu.VMEM((p, nchunk * 8, LANES), F32)],
        compiler_params=_params(("parallel", "parallel"), 56),
        name="gated_delta_net",
    )(b_qkv, b_qkv, b_qkv, conv_w, conv_w, conv_w, a_col, a_rows, b_col, b_z,
      a_log.reshape(B_HEADS, 1, 1), dt_bias.reshape(B_HEADS, 1, 1), norm_g.reshape(1, -1))


def _hybrid_sublayer(x, m, positions, w_in, w_out, conv_w, a_log, dt_bias, norm_g, ln_g, ln_b):
    bsz, seq, _ = x.shape
    o = EVEN_OFFS
    cols = lambda i: w_in[:, o[i]:o[i + 1]]
    w_qki = jnp.concatenate([cols(0), cols(1), cols(3)], axis=1)
    w_ik2 = jnp.concatenate([cols(4), cols(4)], axis=1)
    w_small = jnp.concatenate([cols(5), cols(7), cols(8),
                               jnp.zeros((D_MODEL, LANES - IDX_HEADS - 2 * B_HEADS), w_in.dtype)], axis=1)
    qki, a_vt, ik2, small_t, b_qkv, b_z = _in_projection(
        x, m, [w_qki, cols(2), w_ik2, w_small, cols(6), cols(9)], [BF, BF, BF, F32, F32, F32],
        transposed=(False, True, False, True, False, False),
        scale0=jnp.concatenate([jnp.full((A_WIDTH,), A_HEAD_DIM ** -0.5 * LOG2E, F32),
                                jnp.ones((2 * A_WIDTH,), F32)]))

    out_a = _dsa_attention(qki, a_vt, ik2, small_t[:, :IDX_HEADS], min(TOPK_MAX, seq // 4))

    a_t = small_t[:, IDX_HEADS:IDX_HEADS + B_HEADS]
    b_t = small_t[:, IDX_HEADS + B_HEADS:IDX_HEADS + 2 * B_HEADS]
    out_b = _gdn_mixer(b_qkv, a_t[..., None], a_t.reshape(bsz, B_HEADS, seq // CHUNK, CHUNK),
                       b_t[..., None], b_z, conv_w, a_log, dt_bias, norm_g)

    return m, [out_a, out_b], [w_out[:A_WIDTH], w_out[A_WIDTH:]], ln_g, ln_b


def _rotate_half_cols(w):
    half = C_ROPE // 2
    return jnp.concatenate([-w[..., half:], w[..., :half]], axis=-1)


def _mla_sublayer(x, m, cos, sin, w_in, q_norm_g, w_q_up, kv_norm_g, w_kv_up, w_out, ln_g, ln_b):
    pad = LANES - C_NOPE - C_ROPE
    w_kr = w_in[:, C_Q_RANK + C_KV_RANK:]
    place = lambda w: jnp.pad(w, ((0, 0), (C_NOPE, pad)))
    c_q, c_kv, k_r, k_rot = _in_projection(
        x, m, [w_in[:, :C_Q_RANK], w_in[:, C_Q_RANK:C_Q_RANK + C_KV_RANK], place(w_kr),
               place(_rotate_half_cols(w_kr))], [F32, F32, F32, F32])

    wq = w_q_up.reshape(C_Q_RANK, C_HEADS, C_NOPE + C_ROPE)
    wq_main = jnp.pad(wq, ((0, 0), (0, 0), (0, pad))).reshape(C_Q_RANK, -1)
    wq_rot = jnp.pad(_rotate_half_cols(wq[..., C_NOPE:]), ((0, 0), (0, 0), (C_NOPE, pad))).reshape(C_Q_RANK, -1)
    wkv = w_kv_up.reshape(C_KV_RANK, C_HEADS, C_NOPE + C_V)
    wk = jnp.pad(wkv[..., :C_NOPE], ((0, 0), (0, 0), (0, LANES - C_NOPE))).reshape(C_KV_RANK, -1)
    wv = wkv[..., C_NOPE:].reshape(C_KV_RANK, -1)

    q, k, v = _mla_qkv(c_q, c_kv, k_r, k_rot, cos, sin, q_norm_g, kv_norm_g, wq_main, wq_rot, wk, wv)
    out = _mla_attention(q, k, v)
    return m, [out], [w_out], ln_g, ln_b


def kernel(x, c, positions, mod_w, mod_b, ln_g, ln_b, ffn_w_gate, ffn_w_up, ffn_w_down, hyb_w_in, hyb_w_out, gdn_conv_w, gdn_a_log, gdn_dt_bias, gdn_norm_g, mla_w_in, mla_q_norm_g, mla_w_q_up, mla_kv_norm_g, mla_w_kv_up, mla_w_out):
    bsz = x.shape[0]
    mod = _modulation(c, mod_w, mod_b).reshape(DEPTH, bsz, N_SUB, 3, D_MODEL)
    cos, sin = _rope_tables(positions)
    for layer in range(DEPTH):
        x = _ffn_sublayer(x, mod[layer, :, 0], ffn_w_gate[layer, 0], ffn_w_up[layer, 0], ffn_w_down[layer, 0],
                          ln_g[layer, 0], ln_b[layer, 0])
        if layer % 2 == 0:
            e = layer // 2
            mixer = _hybrid_sublayer(x, mod[layer, :, 1], positions, hyb_w_in[e], hyb_w_out[e], gdn_conv_w[e],
                                     gdn_a_log[e], gdn_dt_bias[e], gdn_norm_g[e], ln_g[layer, 1], ln_b[layer, 1])
        else:
            o = layer // 2
            mixer = _mla_sublayer(x, mod[layer, :, 1], cos, sin, mla_w_in[o], mla_q_norm_g[o], mla_w_q_up[o],
                                  mla_kv_norm_g[o], mla_w_kv_up[o], mla_w_out[o], ln_g[layer, 1], ln_b[layer, 1])
        x = _ffn_sublayer(x, mod[layer, :, 2], ffn_w_gate[layer, 1], ffn_w_up[layer, 1], ffn_w_down[layer, 1],
                          ln_g[layer, 2], ln_b[layer, 2], mixer=mixer)
    return x
```

```python
import functools

import numpy as np
import jax
import jax.numpy as jnp
from jax import lax
from jax.experimental import pallas as pl
from jax.experimental.pallas import tpu as pltpu

BF = jnp.bfloat16
F32 = jnp.float32
I32 = jnp.int32

D_MODEL = 1024
DEPTH = 4
CHUNK = 64
DEEPNORM_ALPHA = (2.0 * DEPTH) ** 0.25
D_FF = 2816
N_SUB = 3
LN_EPS = 1e-5
RMS_EPS = 1e-6
NEG_INF = -1e30

A_HEADS = 8
A_HEAD_DIM = 64
A_WIDTH = A_HEADS * A_HEAD_DIM
IDX_HEADS = 8
IDX_DIM = 64
TOPK_MAX = 256

B_HEADS = 4
B_HEAD_DIM = 128
B_WIDTH = B_HEADS * B_HEAD_DIM
CONV_K = 4

EVEN_SIZES = (A_WIDTH, A_WIDTH, A_WIDTH, IDX_HEADS * IDX_DIM, IDX_DIM, IDX_HEADS,
              3 * B_WIDTH, B_HEADS, B_HEADS, B_WIDTH)
EVEN_OFFS = tuple(int(s) for s in np.cumsum((0,) + EVEN_SIZES))

C_HEADS = 16
C_Q_RANK = 384
C_KV_RANK = 256
C_NOPE = 64
C_ROPE = 32
C_V = 64
ROPE_THETA = 10000.0

LANES = 128
ROW_TILE = 512
MXU_WIDTH = 256
FF_SPLITS = (0, 6 * MXU_WIDTH, D_FF)
ATT_TILE = 256
LOG2E = 1.4426950408889634
INT_MIN = -(2 ** 31)
INT_MAX = 2 ** 31 - 1


def _params(sem, vmem_mb):
    return pltpu.CompilerParams(dimension_semantics=sem, vmem_limit_bytes=vmem_mb << 20)


def _const_spec(shape):
    nd = len(shape)
    return pl.BlockSpec(shape, lambda *_: (0,) * nd, pipeline_mode=pl.Buffered(1))


def _dot(a, b):
    return jnp.dot(a, b, preferred_element_type=F32)


def _dot_nt(a, b):
    return lax.dot_general(a, b, (((1,), (1,)), ((), ())), preferred_element_type=F32)


def _dot_tn(a, b):
    return lax.dot_general(a, b, (((0,), (0,)), ((), ())), preferred_element_type=F32)


def _dot3(a, b):
    a_hi = a.astype(BF)
    b_hi = b.astype(BF)
    a_lo = (a - a_hi.astype(F32)).astype(BF)
    b_lo = (b - b_hi.astype(F32)).astype(BF)
    return _dot(a_hi, b_hi) + (_dot(a_hi, b_lo) + _dot(a_lo, b_hi))


def _silu(x):
    return x * jax.nn.sigmoid(x)


def _layer_norm(z, g, b):
    mu = jnp.mean(z, axis=-1, keepdims=True)
    zc = z - mu
    var = jnp.mean(zc * zc, axis=-1, keepdims=True)
    return zc * lax.rsqrt(var + LN_EPS) * g + b


def _modulate(x, m_ref):
    return x * (1.0 + m_ref[1:2, :]) + m_ref[0:1, :]


def _mod_kernel(c_ref, w_ref, b_ref, o_ref):
    cs = _silu(c_ref[...]).astype(BF)
    o_ref[...] = _dot(cs, w_ref[...].astype(BF)) + b_ref[...]


def _modulation(c, mod_w, mod_b):
    bsz = c.shape[0]
    n = mod_w.shape[-1]
    tn = 1536
    return pl.pallas_call(
        _mod_kernel,
        grid=(DEPTH, n // tn),
        in_specs=[pl.BlockSpec((bsz, D_MODEL), lambda l, j: (0, 0)),
                  pl.BlockSpec((None, D_MODEL, tn), lambda l, j: (l, 0, j)),
                  pl.BlockSpec((None, 1, tn), lambda l, j: (l, 0, j))],
        out_specs=pl.BlockSpec((None, bsz, tn), lambda l, j: (l, 0, j)),
        out_shape=jax.ShapeDtypeStruct((DEPTH, bsz, n), F32),
        compiler_params=_params(("arbitrary", "arbitrary"), 40),
        name="adaln_modulation",
    )(c, mod_w, mod_b.reshape(DEPTH, 1, n))


def _ffn_kernel(*refs, n_mix):
    if n_mix:
        x_ref, mo_ref = refs[:2]
        mix_refs, wo_refs = refs[2:2 + n_mix], refs[2 + n_mix:2 + 2 * n_mix]
        go_ref, bo_ref, m_ref, wg_ref, wu_ref, wd_ref, g_ref, b_ref, o_ref, acc_ref, xin_ref = refs[2 + 2 * n_mix:]
    else:
        xin_ref, m_ref, wg_ref, wu_ref, wd_ref, g_ref, b_ref, o_ref, acc_ref = refs
    tm = o_ref.shape[0]
    halves = [slice(0, tm // 2), slice(tm // 2, tm)]
    if n_mix:
        ys = [sum(_dot(mx[r, :].astype(BF), w[...]) for mx, w in zip(mix_refs, wo_refs)) for r in halves]
        for r, y in zip(halves, ys):
            z = DEEPNORM_ALPHA * x_ref[r, :] + (1.0 + mo_ref[2:3, :]) * y
            xin_ref[r, :] = _layer_norm(z, go_ref[...], bo_ref[...])
    x_ref = xin_ref
    us = [_modulate(x_ref[r, :], m_ref).astype(BF) for r in halves]
    for c, (lo, hi) in enumerate(zip(FF_SPLITS[:-1], FF_SPLITS[1:])):
        sl = slice(lo, hi)
        hg = [_dot(u, wg_ref[:, sl]) for u in us]
        hu = [_dot(u, wu_ref[:, sl]) for u in us]
        hs = [(_silu(a) * b).astype(BF) for a, b in zip(hg, hu)]
        for r, h in zip(halves, hs):
            part = _dot(h, wd_ref[sl, :])
            if c == 0:
                acc_ref[r, :] = part
            else:
                acc_ref[r, :] += part
    for r in halves:
        z = DEEPNORM_ALPHA * x_ref[r, :] + (1.0 + m_ref[2:3, :]) * (0.5 * acc_ref[r, :])
        o_ref[r, :] = _layer_norm(z, g_ref[...], b_ref[...])


def _ffn_sublayer(x, m, wg, wu, wd, ln_g, ln_b, mixer=None):
    bsz, seq, _ = x.shape
    tm = min(ROW_TILE, seq)
    row = lambda width: pl.BlockSpec((None, tm, width), lambda b, i: (b, i, 0))
    mod = pl.BlockSpec((None, 3, D_MODEL), lambda b, i: (b, 0, 0))
    vec = _const_spec((1, D_MODEL))
    ffn_specs = [mod, _const_spec((D_MODEL, D_FF)), _const_spec((D_MODEL, D_FF)), _const_spec((D_FF, D_MODEL)),
                 vec, vec]
    ffn_args = (m, wg.astype(BF), wu.astype(BF), wd.astype(BF), ln_g.reshape(1, -1), ln_b.reshape(1, -1))
    scratch = [pltpu.VMEM((tm, D_MODEL), F32)]
    if mixer is None:
        n_mix, pre_specs, pre_args = 0, [row(D_MODEL)], (x,)
    else:
        m_o, mixes, w_outs, g_o, b_o = mixer
        n_mix = len(mixes)
        pre_specs = ([row(D_MODEL), mod] + [row(mx.shape[-1]) for mx in mixes]
                     + [_const_spec(w.shape) for w in w_outs] + [vec, vec])
        pre_args = (x, m_o, *mixes, *[w.astype(BF) for w in w_outs], g_o.reshape(1, -1), b_o.reshape(1, -1))
        scratch.append(pltpu.VMEM((tm, D_MODEL), F32))
    return pl.pallas_call(
        functools.partial(_ffn_kernel, n_mix=n_mix),
        grid=(bsz, seq // tm),
        in_specs=pre_specs + ffn_specs,
        out_specs=row(D_MODEL),
        out_shape=jax.ShapeDtypeStruct(x.shape, F32),
        scratch_shapes=scratch,
        compiler_params=_params(("parallel", "parallel"), 56),
        name="ffn_sublayer",
    )(*pre_args, *ffn_args)


def _inproj_kernel(x_ref, m_ref, s_ref, *refs, n, transposed):
    u = _modulate(x_ref[...], m_ref).astype(BF)
    for g, (w_ref, o_ref, tr) in enumerate(zip(refs[:n], refs[n:], transposed)):
        y = _dot_nt(w_ref[...], u) if tr else _dot(u, w_ref[...])
        if g == 0:
            y = y * s_ref[...]
        o_ref[...] = y.astype(o_ref.dtype)


def _in_projection(x, m, weights, dtypes, transposed=None, scale0=None):
    bsz, seq, _ = x.shape
    tm = min(ROW_TILE, seq)
    n = len(weights)
    transposed = transposed or (False,) * n
    assert not transposed[0]
    n0 = weights[0].shape[1]
    scale0 = jnp.ones((n0,), F32) if scale0 is None else scale0
    row = lambda width: pl.BlockSpec((None, tm, width), lambda b, i: (b, i, 0))
    colb = lambda width: pl.BlockSpec((None, width, tm), lambda b, i: (b, 0, i))
    ws = [(w.T if tr else w).astype(BF) for w, tr in zip(weights, transposed)]
    return pl.pallas_call(
        functools.partial(_inproj_kernel, n=n, transposed=tuple(transposed)),
        grid=(bsz, seq // tm),
        in_specs=[row(D_MODEL), pl.BlockSpec((None, 3, D_MODEL), lambda b, i: (b, 0, 0)),
                  _const_spec((1, n0))] + [_const_spec(w.shape) for w in ws],
        out_specs=[colb(w.shape[1]) if tr else row(w.shape[1]) for w, tr in zip(weights, transposed)],
        out_shape=[jax.ShapeDtypeStruct((bsz, w.shape[1], seq) if tr else (bsz, seq, w.shape[1]), dt)
                   for w, dt, tr in zip(weights, dtypes, transposed)],
        compiler_params=_params(("parallel", "parallel"), 48),
        name="in_projection",
    )(x, m, scale0.reshape(1, n0), *ws)


def _rope_table_kernel(pos_ref, invf_ref, cos_ref, sin_ref):
    ang = pos_ref[...].astype(F32) * invf_ref[...]
    lane = lax.broadcasted_iota(I32, ang.shape, 1)
    rot = (lane >= C_NOPE) & (lane < C_NOPE + C_ROPE)
    cos_ref[...] = jnp.where(lane < C_NOPE, 1.0, jnp.where(rot, jnp.cos(ang), 0.0))
    sin_ref[...] = jnp.where(rot, jnp.sin(ang), 0.0)


def _rope_tables(positions):
    bsz, seq = positions.shape
    tm = min(ROW_TILE, seq)
    half = C_ROPE // 2
    inv_freq = ROPE_THETA ** (-jnp.arange(half, dtype=F32) / half)
    invf = jnp.zeros((1, LANES), F32).at[0, C_NOPE:C_NOPE + C_ROPE].set(jnp.tile(inv_freq, 2))
    tab = jax.ShapeDtypeStruct((bsz, seq, LANES), F32)
    return pl.pallas_call(
        _rope_table_kernel,
        grid=(bsz, seq // tm),
        in_specs=[pl.BlockSpec((None, tm, 1), lambda b, i: (b, i, 0)), _const_spec((1, LANES))],
        out_specs=[pl.BlockSpec((None, tm, LANES), lambda b, i: (b, i, 0))] * 2,
        out_shape=[tab, tab],
        compiler_params=_params(("parallel", "parallel"), 32),
        name="rope_tables",
    )(positions.reshape(bsz, seq, 1), invf)


def _rms_norm(x, g):
    return x * lax.rsqrt(jnp.mean(x * x, axis=-1, keepdims=True) + RMS_EPS) * g


def _mla_qkv_kernel(cq_ref, ckv_ref, kr_ref, krot_ref, cos_ref, sin_ref, qg_ref, kvg_ref,
                    wq_ref, wqrot_ref, wk_ref, wvt_ref, q_ref, k_ref, vt_ref, *, scale):
    cos = cos_ref[...]
    sin = sin_ref[...]
    cq = _rms_norm(cq_ref[...], qg_ref[...]).astype(BF)
    ckv = _rms_norm(ckv_ref[...], kvg_ref[...]).astype(BF)
    k_rope = kr_ref[...] * cos + krot_ref[...] * sin
    qcos = cos * scale
    qsin = sin * scale
    per = MXU_WIDTH // LANES
    for g in range(C_HEADS // per):
        wide = slice(g * MXU_WIDTH, (g + 1) * MXU_WIDTH)
        q_main, q_rot, k_nope = _dot(cq, wq_ref[:, wide]), _dot(cq, wqrot_ref[:, wide]), _dot(ckv, wk_ref[:, wide])
        for r in range(per):
            part = slice(r * LANES, (r + 1) * LANES)
            sl = slice(g * MXU_WIDTH + r * LANES, g * MXU_WIDTH + (r + 1) * LANES)
            q_ref[:, sl] = (q_main[:, part] * qcos + q_rot[:, part] * qsin).astype(BF)
            k_ref[:, sl] = (k_nope[:, part] + k_rope).astype(BF)
    vt_ref[...] = _dot_nt(wvt_ref[...], ckv).astype(BF)


def _mla_qkv(c_q, c_kv, k_r, k_rot, cos, sin, qg, kvg, wq, wqrot, wk, wv):
    bsz, seq, _ = c_q.shape
    tm = min(ROW_TILE, seq)
    row = lambda width: pl.BlockSpec((None, tm, width), lambda b, i: (b, i, 0))
    hw = C_HEADS * LANES
    vw = C_HEADS * C_V
    return pl.pallas_call(
        functools.partial(_mla_qkv_kernel, scale=(C_NOPE + C_ROPE) ** -0.5 * LOG2E),
        grid=(bsz, seq // tm),
        in_specs=[row(C_Q_RANK), row(C_KV_RANK), row(LANES), row(LANES), row(LANES), row(LANES),
                  _const_spec((1, C_Q_RANK)), _const_spec((1, C_KV_RANK)),
                  _const_spec(wq.shape), _const_spec(wqrot.shape), _const_spec(wk.shape),
                  _const_spec((vw, C_KV_RANK))],
        out_specs=[row(hw), row(hw), pl.BlockSpec((None, vw, tm), lambda b, i: (b, 0, i))],
        out_shape=[jax.ShapeDtypeStruct((bsz, seq, hw), BF),
                   jax.ShapeDtypeStruct((bsz, seq, hw), BF),
                   jax.ShapeDtypeStruct((bsz, vw, seq), BF)],
        compiler_params=_params(("parallel", "parallel"), 40),
        name="mla_qkv",
    )(c_q, c_kv, k_r, k_rot, cos, sin, qg.reshape(1, -1), kvg.reshape(1, -1),
      wq.astype(BF), wqrot.astype(BF), wk.astype(BF), wv.T.astype(BF))


def _softmax_heads(logits_fn, values_fn, nheads, dv, m_ref, l_ref, acc_ref):
    ahead = 3
    pending = [logits_fn(h) for h in range(min(ahead, nheads))]
    for h in range(nheads):
        s = pending.pop(0)
        if h + ahead < nheads:
            pending.append(logits_fn(h + ahead))
        rows = slice(h * dv, (h + 1) * dv)
        m_old = m_ref[h]
        m_new = jnp.maximum(m_old, jnp.max(s, axis=0, keepdims=True))
        alpha = jnp.exp2(m_old - m_new)
        p = jnp.exp2(s - m_new[0:1, :])
        l_ref[h] = alpha * l_ref[h] + jnp.sum(p, axis=0, keepdims=True)
        m_ref[h] = m_new
        acc_ref[rows, :] = alpha[0:1, :] * acc_ref[rows, :] + _dot(values_fn(h), p.astype(BF))


def _softmax_finish(o_ref, l_ref, acc_ref, nheads, dv):
    per = LANES // dv
    for j in range(nheads // per):
        parts = [acc_ref[(per * j + r) * dv:(per * j + r + 1) * dv, :] / l_ref[per * j + r][0:1, :]
                 for r in range(per)]
        o_ref[:, j * LANES:(j + 1) * LANES] = jnp.concatenate(parts, axis=0).T.astype(o_ref.dtype)


MLA_GROUP = 16


def _mla_attn_kernel(q_ref, k_ref, vt_ref, o_ref, m_ref, l_ref, acc_ref):
    t = ATT_TILE
    i = pl.program_id(2)
    rk = lax.broadcasted_iota(I32, (t, t), 0)
    cq = lax.broadcasted_iota(I32, (t, t), 1)
    diag_bias = jnp.where((rk >> 6) <= (cq >> 6), 0.0, NEG_INF)
    m_ref[...] = jnp.full(m_ref.shape, NEG_INF, F32)
    l_ref[...] = jnp.zeros(l_ref.shape, F32)
    acc_ref[...] = jnp.zeros(acc_ref.shape, F32)

    def step(kt, bias):
        ks = pl.multiple_of(kt * t, t)

        def logits(h):
            hs = slice(h * LANES, (h + 1) * LANES)
            s = _dot_nt(k_ref[pl.ds(ks, t), hs], q_ref[:, hs])
            return s if bias is None else s + bias

        _softmax_heads(logits, lambda h: vt_ref[h * C_V:(h + 1) * C_V, pl.ds(ks, t)],
                       MLA_GROUP, C_V, m_ref, l_ref, acc_ref)

    def body(kt, carry):
        step(kt, None)
        return carry

    lax.fori_loop(0, i, body, 0)
    step(i, diag_bias)
    _softmax_finish(o_ref, l_ref, acc_ref, MLA_GROUP, C_V)


def _mla_attention(q, k, v_t):
    bsz, seq, _ = q.shape
    t = ATT_TILE
    g = MLA_GROUP
    return pl.pallas_call(
        _mla_attn_kernel,
        grid=(bsz, C_HEADS // g, seq // t),
        in_specs=[pl.BlockSpec((None, t, g * LANES), lambda b, h, i: (b, i, h)),
                  pl.BlockSpec((None, seq, g * LANES), lambda b, h, i: (b, 0, h)),
                  pl.BlockSpec((None, g * C_V, seq), lambda b, h, i: (b, h, 0))],
        out_specs=pl.BlockSpec((None, t, g * C_V), lambda b, h, i: (b, i, h)),
        out_shape=jax.ShapeDtypeStruct((bsz, seq, C_HEADS * C_V), BF),
        scratch_shapes=[pltpu.VMEM((g, 8, t), F32), pltpu.VMEM((g, 8, t), F32), pltpu.VMEM((g * C_V, t), F32)],
        compiler_params=_params(("parallel", "parallel", "arbitrary"), 48),
        name="mla_attention",
    )(q, k, v_t)


def _order_key(s):
    b = lax.bitcast_convert_type(s, I32)
    return b ^ ((b >> 31) & INT_MAX)


def _dsa_kernel(q_ref, k_ref, iq_ref, vt_ref, ik_ref, wt_ref, o_ref, key_ref, bias_ref, thr_ref,
                m_ref, l_ref, acc_ref, *, topk):
    t = ATT_TILE
    c = pl.program_id(1)
    nk = c + 1
    rk = lax.broadcasted_iota(I32, (t, t), 0)
    cq = lax.broadcasted_iota(I32, (t, t), 1)
    vis_diag = (rk >> 6) <= (cq >> 6)
    low_half = lax.broadcasted_iota(I32, (t, LANES), 1) < A_HEAD_DIM

    def head_of_pair(ref, h):
        blk = ref[:, (h // 2) * LANES:(h // 2 + 1) * LANES]
        keep = low_half if h % 2 == 0 else jnp.logical_not(low_half)
        return jnp.where(keep, blk, jnp.zeros_like(blk))

    coef = wt_ref[...] * (IDX_HEADS ** -0.5 * IDX_DIM ** -0.5)
    iq = [head_of_pair(iq_ref, h) for h in range(IDX_HEADS)]

    def score_tile(kt, carry):
        ks = pl.multiple_of(kt * t, t)
        ik = ik_ref[pl.ds(ks, t), :]
        s = jnp.zeros((t, t), F32)
        for h in range(IDX_HEADS):
            s = s + jnp.maximum(_dot_nt(ik, iq[h]), 0.0) * coef[h:h + 1, :]
        s = jnp.where(kt < c, s, jnp.where(vis_diag, s, NEG_INF))
        key_ref[pl.ds(ks, t), :] = _order_key(s)
        return carry

    lax.fori_loop(0, nk, score_tile, 0)

    thr_ref[0:1, :] = jnp.full((1, t), INT_MIN, I32)
    thr_ref[1:2, :] = jnp.full((1, t), INT_MAX, I32)

    def count(pred):
        def body(kt, acc):
            ks = pl.multiple_of(kt * t, t)
            hit = pred(key_ref[pl.ds(ks, t), :], rk + kt * t)
            return acc + jnp.sum(jnp.where(hit, 1, 0).reshape(t // 8, 8, t), axis=0)
        acc = lax.fori_loop(0, nk, body, jnp.zeros((8, t), I32))
        return jnp.sum(acc, axis=0, keepdims=True)

    @pl.when(c > 0)
    def _search():
        n_nonneg = count(lambda keys, idx: keys >= 0)
        thr0 = jnp.where(n_nonneg >= topk, 0, INT_MIN)

        def value_bit(j, thr):
            cand = thr + (jnp.int32(1) << (30 - j))
            n = count(lambda keys, idx: keys >= cand)
            return jnp.where(n >= topk, cand, thr)

        thr = lax.fori_loop(0, 31, value_bit, thr0)
        n_gt = count(lambda keys, idx: keys > thr)
        n_eq = count(lambda keys, idx: keys == thr)
        need = topk - n_gt
        thr_ref[0:1, :] = thr

        @pl.when(jnp.max(n_eq - need) > 0)
        def _ties():
            def index_bit(j, x):
                cand = x + (jnp.int32(1) << (11 - j))
                n = count(lambda keys, idx: (keys == thr) & (idx < cand))
                return jnp.where(n < need, cand, x)
            thr_ref[1:2, :] = lax.fori_loop(0, 12, index_bit, jnp.zeros((1, t), I32))

    thr = thr_ref[0:1, :]
    last = thr_ref[1:2, :]

    def bias_tile(kt, carry):
        ks = pl.multiple_of(kt * t, t)
        keys = key_ref[pl.ds(ks, t), :]
        tie = jnp.where(keys == thr, jnp.where(rk + kt * t <= last, 0.0, NEG_INF), NEG_INF)
        b = jnp.where(keys > thr, 0.0, tie)
        bias_ref[pl.ds(ks, t), :] = jnp.where(kt < c, b, jnp.where(vis_diag, b, NEG_INF))
        return carry

    lax.fori_loop(0, nk, bias_tile, 0)

    rel = (cq - rk).astype(F32)
    qs = [head_of_pair(q_ref, h) for h in range(A_HEADS)]
    m_ref[...] = jnp.full(m_ref.shape, -1e20, F32)
    l_ref[...] = jnp.zeros(l_ref.shape, F32)
    acc_ref[...] = jnp.zeros(acc_ref.shape, F32)

    def attend(kt, carry):
        ks = pl.multiple_of(kt * t, t)
        dist = jnp.abs(rel + ((c - kt) * t).astype(F32))
        bias = bias_ref[pl.ds(ks, t), :]

        def logits(h):
            slope = 2.0 ** (-8.0 * (h + 1) / A_HEADS) * LOG2E
            hp = slice((h // 2) * LANES, (h // 2 + 1) * LANES)
            return _dot_nt(k_ref[pl.ds(ks, t), hp], qs[h]) + (bias - slope * dist)

        _softmax_heads(logits, lambda h: vt_ref[h * A_HEAD_DIM:(h + 1) * A_HEAD_DIM, pl.ds(ks, t)],
                       A_HEADS, A_HEAD_DIM, m_ref, l_ref, acc_ref)
        return carry

    lax.fori_loop(0, nk, attend, 0)
    _softmax_finish(o_ref, l_ref, acc_ref, A_HEADS, A_HEAD_DIM)


def _dsa_attention(qki, v_t, ik2, w_t, topk):
    bsz, seq, _ = qki.shape
    t = ATT_TILE
    assert topk == t and seq % t == 0 and t == 4 * CHUNK
    return pl.pallas_call(
        functools.partial(_dsa_kernel, topk=topk),
        grid=(bsz, seq // t),
        in_specs=[pl.BlockSpec((None, t, A_WIDTH), lambda b, c: (b, c, 0)),
                  pl.BlockSpec((None, seq, A_WIDTH), lambda b, c: (b, 0, 1)),
                  pl.BlockSpec((None, t, A_WIDTH), lambda b, c: (b, c, 2)),
                  pl.BlockSpec((None, A_WIDTH, seq), lambda b, c: (b, 0, 0)),
                  pl.BlockSpec((None, seq, LANES), lambda b, c: (b, 0, 0)),
                  pl.BlockSpec((None, IDX_HEADS, t), lambda b, c: (b, 0, c))],
        out_specs=pl.BlockSpec((None, t, A_WIDTH), lambda b, c: (b, c, 0)),
        out_shape=jax.ShapeDtypeStruct((bsz, seq, A_WIDTH), BF),
        scratch_shapes=[pltpu.VMEM((seq, t), I32), pltpu.VMEM((seq, t), F32), pltpu.VMEM((8, t), I32),
                        pltpu.VMEM((A_HEADS, 8, t), F32), pltpu.VMEM((A_HEADS, 8, t), F32),
                        pltpu.VMEM((A_WIDTH, t), F32)],
        compiler_params=_params(("parallel", "arbitrary"), 40),
        name="dsa_attention",
    )(qki, qki, qki, v_t, ik2, w_t)


GDN_UNROLL = 8


GDN_PAIR = 2


def _gdn_kernel(xq_ref, xk_ref, xv_ref, cwq_ref, cwk_ref, cwv_ref, a_ref, arow_ref, b_ref, z_ref,
                alog_ref, dtb_ref, ng_ref, o_ref,
                q_s, k_s, v_s, gc_s, grow_s, u_s, wq_s, kd_s, at_s, egl_s):
    seq = xq_ref.shape[0]
    nchunk = seq // CHUNK
    cs = CHUNK
    heads = range(GDN_PAIR)
    lanes_of = lambda h: slice(h * LANES, (h + 1) * LANES)

    row = lax.broadcasted_iota(I32, (seq, LANES), 0)

    def conv_silu(x_ref, cw_ref, h):
        x = x_ref[:, lanes_of(h)]
        y = x * cw_ref[CONV_K - 1:CONV_K, lanes_of(h)]
        for j in range(1, CONV_K):
            shifted = jnp.where(row >= j, pltpu.roll(x, j, 0), 0.0)
            y = y + shifted * cw_ref[CONV_K - 1 - j:CONV_K - j, lanes_of(h)]
        return _silu(y)

    def l2n(x):
        return x * lax.rsqrt(jnp.sum(x * x, axis=-1, keepdims=True) + RMS_EPS)

    def log_decay(a, h):
        pre = a + dtb_ref[h]
        softplus = jnp.maximum(pre, 0.0) + jnp.log(1.0 + jnp.exp(-jnp.abs(pre)))
        return -jnp.exp(alog_ref[h]) * softplus

    ri = lax.broadcasted_iota(I32, (cs, cs), 0)
    ci = lax.broadcasted_iota(I32, (cs, cs), 1)
    incl = ri >= ci
    strict = ri > ci
    eye = jnp.where(ri == ci, 1.0, 0.0)
    pos = row & (cs - 1)

    for h in heads:
        q_s[h] = l2n(conv_silu(xq_ref, cwq_ref, h)) * (B_HEAD_DIM ** -0.5)
        k_s[h] = l2n(conv_silu(xk_ref, cwk_ref, h))
        v_s[h] = conv_silu(xv_ref, cwv_ref, h)
        gc = jnp.broadcast_to(log_decay(a_ref[h], h), (seq, LANES))
        for sh in (1, 2, 4, 8, 16, 32):
            gc = gc + jnp.where(pos >= sh, pltpu.roll(gc, sh, 0), 0.0)
        gc_s[h] = gc
        grow_s[h] = _dot3(log_decay(arow_ref[h], h), jnp.where(ri <= ci, 1.0, 0.0))

    def chunk_rows(n):
        return pl.ds(pl.multiple_of(n * cs, cs), cs)

    def local_group(it, carry):
        ns = [it * GDN_UNROLL + j for j in range(GDN_UNROLL)] * GDN_PAIR
        hs = [h for h in heads for _ in range(GDN_UNROLL)]
        each = lambda f, *lists: [f(*args) for args in zip(*lists)]
        rows = [chunk_rows(n) for n in ns]
        q = each(lambda h, r: q_s[h, r, :], hs, rows)
        k = each(lambda h, r: k_s[h, r, :], hs, rows)
        beta = each(lambda h, r: jax.nn.sigmoid(b_ref[h, r, :]), hs, rows)
        gcol = each(lambda h, r: gc_s[h, r, :], hs, rows)
        grow = each(lambda h, n: grow_s[h, pl.ds(n, 1), :], hs, ns)
        decay = each(lambda gc, gr: jnp.exp(jnp.where(incl, gc[:, :cs] - gr, NEG_INF)), gcol, grow)
        kb = each(lambda a, b: a * b, k, beta)
        kbf = [a.astype(BF) for a in k]
        kk = each(lambda a, b: _dot_nt(a.astype(BF), b), kb, kbf)
        x = each(lambda a, d: -jnp.where(strict, a * d, 0.0), kk, decay)
        tinv = [eye + a for a in x]
        for _ in range(5):
            x = [_dot3(a, a) for a in x]
            tinv = each(lambda tj, a: tj + _dot3(tj, a), tinv, x)
        tb = [a.astype(BF) for a in tinv]
        egc = [jnp.exp(a) for a in gcol]
        vb = each(lambda h, r, b: (v_s[h, r, :] * b).astype(BF), hs, rows, beta)
        u = each(_dot, tb, vb)
        w = each(lambda tj, a, e: _dot(tj, (a * e).astype(BF)), tb, kb, egc)
        attn = each(lambda a, b, d: jnp.where(incl, _dot_nt(a.astype(BF), b) * d, 0.0), q, kbf, decay)
        for j, (h, n) in enumerate(zip(hs, ns)):
            glast = gcol[j][cs - 1:cs, :]
            u_s[h, rows[j], :] = u[j]
            wq_s[h, pl.ds(pl.multiple_of(n * 2 * cs, 2 * cs), cs), :] = w[j].astype(BF)
            wq_s[h, pl.ds(pl.multiple_of(n * 2 * cs + cs, cs), cs), :] = (q[j] * egc[j]).astype(BF)
            at_s[h, rows[j], :] = attn[j].astype(BF)
            kd_s[h, rows[j], :] = (k[j] * jnp.exp(glast - gcol[j])).astype(BF)
            egl_s[h, pl.ds(pl.multiple_of(n * 8, 8), 8), :] = jnp.broadcast_to(jnp.exp(glast), (8, LANES))
        return carry

    lax.fori_loop(0, nchunk // GDN_UNROLL, local_group, 0)

    ng = ng_ref[...]

    def scan(n, states):
        rows = chunk_rows(n)
        sb = [s.astype(BF) for s in states]
        r = [_dot(wq_s[h, pl.ds(pl.multiple_of(n * 2 * cs, 2 * cs), 2 * cs), :], sb[h]) for h in heads]
        v_new = [(u_s[h, rows, :] - r[h][:cs, :]).astype(BF) for h in heads]
        o = [r[h][cs:, :] + _dot(at_s[h, rows, :], v_new[h]) for h in heads]
        upd = [_dot_tn(kd_s[h, rows, :], v_new[h]) for h in heads]
        new_states = []
        for h in heads:
            egl = egl_s[h, pl.ds(pl.multiple_of(n * 8, 8), 1), :]
            new_states.append(states[h] * egl + upd[h])
            on = o[h] * lax.rsqrt(jnp.mean(o[h] * o[h], axis=-1, keepdims=True) + RMS_EPS) * ng
            o_ref[rows, lanes_of(h)] = on * _silu(z_ref[rows, lanes_of(h)])
        return tuple(new_states)

    lax.fori_loop(0, nchunk, scan, tuple(jnp.zeros((B_HEAD_DIM, B_HEAD_DIM), F32) for _ in heads))


def _gdn_mixer(b_qkv, a_col, a_rows, b_col, b_z, conv_w, a_log, dt_bias, norm_g):
    bsz, seq, _ = b_qkv.shape
    nchunk = seq // CHUNK
    assert nchunk % GDN_UNROLL == 0
    p, npair = GDN_PAIR, B_HEADS // GDN_PAIR
    pw = p * LANES
    col = lambda off: pl.BlockSpec((None, seq, pw), lambda b, h: (b, 0, h + off))
    cw = lambda off: pl.BlockSpec((CONV_K, pw), lambda b, h: (0, h + off))
    gate = pl.BlockSpec((None, p, seq, 1), lambda b, h: (b, h, 0, 0))
    gate_rows = pl.BlockSpec((None, p, nchunk, CHUNK), lambda b, h: (b, h, 0, 0))
    scal = pl.BlockSpec((p, 1, 1), lambda b, h: (h, 0, 0))
    big = pltpu.VMEM((p, seq, LANES), F32)
    big_bf = pltpu.VMEM((p, seq, LANES), BF)
    return pl.pallas_call(
        _gdn_kernel,
        grid=(bsz, npair),
        in_specs=[col(0), col(npair), col(2 * npair), cw(0), cw(npair), cw(2 * npair),
                  gate, gate_rows, gate, col(0), scal, scal, _const_spec((1, B_HEAD_DIM))],
        out_specs=col(0),
        out_shape=jax.ShapeDtypeStruct((bsz, seq, B_WIDTH), F32),
        scratch_shapes=[big, big, big, big, pltpu.VMEM((p, nchunk, CHUNK), F32),
                        big, pltpu.VMEM((p, 2 * seq, LANES), BF), big_bf,
                        pltpu.VMEM((p, seq, CHUNK), BF), pltpu.VMEM((p, nchunk * 8, LANES), F32)],
        compiler_params=_params(("parallel", "parallel"), 56),
        name="gated_delta_net",
    )(b_qkv, b_qkv, b_qkv, conv_w, conv_w, conv_w, a_col, a_rows, b_col, b_z,
      a_log.reshape(B_HEADS, 1, 1), dt_bias.reshape(B_HEADS, 1, 1), norm_g.reshape(1, -1))


def _hybrid_sublayer(x, m, positions, w_in, w_out, conv_w, a_log, dt_bias, norm_g, ln_g, ln_b):
    bsz, seq, _ = x.shape
    o = EVEN_OFFS
    cols = lambda i: w_in[:, o[i]:o[i + 1]]
    w_qki = jnp.concatenate([cols(0), cols(1), cols(3)], axis=1)
    w_ik2 = jnp.concatenate([cols(4), cols(4)], axis=1)
    w_small = jnp.concatenate([cols(5), cols(7), cols(8),
                               jnp.zeros((D_MODEL, LANES - IDX_HEADS - 2 * B_HEADS), w_in.dtype)], axis=1)
    qki, a_vt, ik2, small_t, b_qkv, b_z = _in_projection(
        x, m, [w_qki, cols(2), w_ik2, w_small, cols(6), cols(9)], [BF, BF, BF, F32, F32, F32],
        transposed=(False, True, False, True, False, False),
        scale0=jnp.concatenate([jnp.full((A_WIDTH,), A_HEAD_DIM ** -0.5 * LOG2E, F32),
                                jnp.ones((2 * A_WIDTH,), F32)]))

    out_a = _dsa_attention(qki, a_vt, ik2, small_t[:, :IDX_HEADS], min(TOPK_MAX, seq // 4))

    a_t = small_t[:, IDX_HEADS:IDX_HEADS + B_HEADS]
    b_t = small_t[:, IDX_HEADS + B_HEADS:IDX_HEADS + 2 * B_HEADS]
    out_b = _gdn_mixer(b_qkv, a_t[..., None], a_t.reshape(bsz, B_HEADS, seq // CHUNK, CHUNK),
                       b_t[..., None], b_z, conv_w, a_log, dt_bias, norm_g)

    return m, [out_a, out_b], [w_out[:A_WIDTH], w_out[A_WIDTH:]], ln_g, ln_b


def _rotate_half_cols(w):
    half = C_ROPE // 2
    return jnp.concatenate([-w[..., half:], w[..., :half]], axis=-1)


def _mla_sublayer(x, m, cos, sin, w_in, q_norm_g, w_q_up, kv_norm_g, w_kv_up, w_out, ln_g, ln_b):
    pad = LANES - C_NOPE - C_ROPE
    w_kr = w_in[:, C_Q_RANK + C_KV_RANK:]
    place = lambda w: jnp.pad(w, ((0, 0), (C_NOPE, pad)))
    c_q, c_kv, k_r, k_rot = _in_projection(
        x, m, [w_in[:, :C_Q_RANK], w_in[:, C_Q_RANK:C_Q_RANK + C_KV_RANK], place(w_kr),
               place(_rotate_half_cols(w_kr))], [F32, F32, F32, F32])

    wq = w_q_up.reshape(C_Q_RANK, C_HEADS, C_NOPE + C_ROPE)
    wq_main = jnp.pad(wq, ((0, 0), (0, 0), (0, pad))).reshape(C_Q_RANK, -1)
    wq_rot = jnp.pad(_rotate_half_cols(wq[..., C_NOPE:]), ((0, 0), (0, 0), (C_NOPE, pad))).reshape(C_Q_RANK, -1)
    wkv = w_kv_up.reshape(C_KV_RANK, C_HEADS, C_NOPE + C_V)
    wk = jnp.pad(wkv[..., :C_NOPE], ((0, 0), (0, 0), (0, LANES - C_NOPE))).reshape(C_KV_RANK, -1)
    wv = wkv[..., C_NOPE:].reshape(C_KV_RANK, -1)

    q, k, v = _mla_qkv(c_q, c_kv, k_r, k_rot, cos, sin, q_norm_g, kv_norm_g, wq_main, wq_rot, wk, wv)
    out = _mla_attention(q, k, v)
    return m, [out], [w_out], ln_g, ln_b


def kernel(x, c, positions, mod_w, mod_b, ln_g, ln_b, ffn_w_gate, ffn_w_up, ffn_w_down, hyb_w_in, hyb_w_out, gdn_conv_w, gdn_a_log, gdn_dt_bias, gdn_norm_g, mla_w_in, mla_q_norm_g, mla_w_q_up, mla_kv_norm_g, mla_w_kv_up, mla_w_out):
    bsz = x.shape[0]
    mod = _modulation(c, mod_w, mod_b).reshape(DEPTH, bsz, N_SUB, 3, D_MODEL)
    cos, sin = _rope_tables(positions)
    for layer in range(DEPTH):
        x = _ffn_sublayer(x, mod[layer, :, 0], ffn_w_gate[layer, 0], ffn_w_up[layer, 0], ffn_w_down[layer, 0],
                          ln_g[layer, 0], ln_b[layer, 0])
        if layer % 2 == 0:
            e = layer // 2
            mixer = _hybrid_sublayer(x, mod[layer, :, 1], positions, hyb_w_in[e], hyb_w_out[e], gdn_conv_w[e],
                                     gdn_a_log[e], gdn_dt_bias[e], gdn_norm_g[e], ln_g[layer, 1], ln_b[layer, 1])
        else:
            o = layer // 2
            mixer = _mla_sublayer(x, mod[layer, :, 1], cos, sin, mla_w_in[o], mla_q_norm_g[o], mla_w_q_up[o],
                                  mla_kv_norm_g[o], mla_w_kv_up[o], mla_w_out[o], ln_g[layer, 1], ln_b[layer, 1])
        x = _ffn_sublayer(x, mod[layer, :, 2], ffn_w_gate[layer, 1], ffn_w_up[layer, 1], ffn_w_down[layer, 1],
                          ln_g[layer, 2], ln_b[layer, 2], mixer=mixer)
    return x
```
